```python
import math
import jax
import jax.numpy as jnp
from jax import lax
import numpy as np

D_MODEL = 1024
BATCH = 8
SEQ = 8192
DEPTH = 1

DA_HEADS = 4
DA_HEAD_DIM = 64
DA_V_DIM = 2 * DA_HEAD_DIM
GLA_HEADS = 4
GLA_DK = 64
GLA_DV = 128
GLA_GATE_RANK = 16
GLA_TAU = 16.0
GLA_CHUNK = 64
MEM_LEN = 256
XA_HEADS = 4
XA_HEAD_DIM = 128
N_BRANCH = 3
PEER_HEADS = 8
PEER_N_KEYS = 128
PEER_N_EXPERTS = PEER_N_KEYS * PEER_N_KEYS
PEER_QDIM = 256
PEER_TOPK = 16
PEER_TOKEN_BLOCK = 128
Q_BLOCK = 128
RMS_EPS = 1e-6
D_IN_PROJ = (3 * DA_HEADS * 2 * DA_HEAD_DIM + 2 * GLA_HEADS * GLA_DK + 2 * GLA_HEADS * GLA_DV + GLA_GATE_RANK + XA_HEADS * XA_HEAD_DIM + N_BRANCH * D_MODEL)

kernel_name = "hybrid_diffattn_gla_memxattn_peer"


def _in_proj_splits():
    sizes = [DA_HEADS * 2 * DA_HEAD_DIM, DA_HEADS * 2 * DA_HEAD_DIM, DA_HEADS * DA_V_DIM,
             GLA_HEADS * GLA_DK, GLA_HEADS * GLA_DK, GLA_HEADS * GLA_DV, GLA_HEADS * GLA_DV, GLA_GATE_RANK,
             XA_HEADS * XA_HEAD_DIM, N_BRANCH * D_MODEL]
    splits, acc = [], 0
    for s in sizes[:-1]:
        acc += s
        splits.append(acc)
    return splits


def _rmsnorm(t, w):
    tf = t.astype(jnp.float32)
    y = tf * lax.rsqrt(jnp.mean(tf * tf, axis=-1, keepdims=True) + RMS_EPS)
    return (y * w.astype(jnp.float32)).astype(t.dtype)


def _alibi_slopes(n_heads):
    return jnp.exp2(-8.0 * jnp.arange(1, n_heads + 1, dtype=jnp.float32) / n_heads)


def _diff_attention(q, k, v, lam, slopes):
    B, H, _, S, d = q.shape
    nblk = S // Q_BLOCK
    kpos = jnp.arange(S, dtype=jnp.int32)
    scale = d ** -0.5
    q_blocks = q.reshape(B, H, 2, nblk, Q_BLOCK, d).transpose(3, 0, 1, 2, 4, 5)

    def block(args):
        q_blk, i = args
        qpos = i * Q_BLOCK + jnp.arange(Q_BLOCK, dtype=jnp.int32)
        dist = qpos[:, None] - kpos[None, :]
        bias = -slopes[:, None, None] * dist.astype(jnp.float32)[None]
        s = jnp.einsum('bhmqd,bhmkd->bhmqk', q_blk, k).astype(jnp.float32) * scale + bias[None, :, None]
        s = jnp.where(dist[None, None, None] >= 0, s, -jnp.inf)
        p = jax.nn.softmax(s, axis=-1)
        w = p[:, :, 0] - lam * p[:, :, 1]
        return jnp.einsum('bhqk,bhkv->bhqv', w.astype(v.dtype), v)

    out = lax.map(block, (q_blocks, jnp.arange(nblk, dtype=jnp.int32)))
    return out.transpose(1, 2, 0, 3, 4).reshape(B, H, S, v.shape[-1])


def _gla(q, k, v, log_a):
    B, S, H, dk = q.shape
    dv = v.shape[-1]
    C = GLA_CHUNK
    n = S // C
    out_dtype = v.dtype

    def to_chunks(t):
        return t.astype(jnp.float32).reshape(B, n, C, H, t.shape[-1]).transpose(1, 0, 3, 2, 4)

    qc, kc, vc, gc = to_chunks(q * (dk ** -0.5)), to_chunks(k), to_chunks(v), to_chunks(log_a)
    causal = jnp.tril(jnp.ones((C, C), dtype=bool))

    def step(state, inp):
        qt, kt, vt, gt = inp
        b = jnp.cumsum(gt, axis=2)
        o_inter = jnp.einsum('bhcd,bhde->bhce', qt * jnp.exp(b), state)
        diff = b[:, :, :, None, :] - b[:, :, None, :, :]
        decay = jnp.exp(jnp.where(causal[:, :, None], diff, -jnp.inf))
        a = jnp.einsum('bhtd,bhsd,bhtsd->bhts', qt, kt, decay)
        o_intra = jnp.einsum('bhts,bhse->bhte', a, vt)
        b_last = b[:, :, -1:, :]
        k_dec = kt * jnp.exp(b_last - b)
        state = state * jnp.exp(b_last[:, :, 0, :])[..., None] + jnp.einsum('bhcd,bhce->bhde', k_dec, vt)
        return state, o_inter + o_intra

    state0 = jnp.zeros((B, H, dk, dv), jnp.float32)
    _, o = lax.scan(step, state0, (qc, kc, vc, gc))
    return o.transpose(1, 0, 3, 2, 4).reshape(B, S, H, dv).astype(out_dtype)


def _mem_attention(q, k, v):
    d = q.shape[-1]
    s = jnp.einsum('bshd,bmhd->bhsm', q, k).astype(jnp.float32) * (d ** -0.5)
    p = jax.nn.softmax(s, axis=-1)
    return jnp.einsum('bhsm,bmhd->bshd', p.astype(v.dtype), v)


def _peer(xn, w_q, sub_keys, expert_u, expert_v):
    B, S, D = xn.shape
    T = PEER_TOKEN_BLOCK
    K = PEER_TOPK
    H = PEER_HEADS
    x_blocks = xn.reshape(-1, T, D)

    def block(xb):
        q = (xb @ w_q).reshape(T, H, 2, PEER_QDIM // 2)
        s = jnp.einsum('thpc,pnc->thpn', q, sub_keys).astype(jnp.float32)
        top_s, top_i = lax.top_k(s, K)
        cand_s = top_s[:, :, 0, :, None] + top_s[:, :, 1, None, :]
        cand_i = top_i[:, :, 0, :, None] * PEER_N_KEYS + top_i[:, :, 1, None, :]
        best_s, best_j = lax.top_k(cand_s.reshape(T, H, K * K), K)
        idx = jnp.take_along_axis(cand_i.reshape(T, H, K * K), best_j, axis=-1)
        g = jax.nn.softmax(best_s, axis=-1)
        u = expert_u[idx]
        v = expert_v[idx]
        act = jax.nn.gelu(jnp.einsum('thkd,td->thk', u, xb).astype(jnp.float32), approximate=False)
        return jnp.einsum('thk,thkd->td', (g * act).astype(xb.dtype), v)

    return lax.map(block, x_blocks).reshape(B, S, D)


def setup_inputs(seed: int = 0) -> dict:
    key = jax.random.key(seed)
    ks = iter(jax.random.split(key, 32))
    L, D = DEPTH, D_MODEL
    f32 = jnp.float32

    def nrm(shape, scale):
        return jax.random.normal(next(ks), shape, f32) * scale

    def gain(shape):
        return 1.0 + 0.02 * jax.random.normal(next(ks), shape, f32)

    da_w = DA_HEADS * DA_V_DIM
    gla_w = GLA_HEADS * GLA_DV
    xa_w = XA_HEADS * XA_HEAD_DIM
    return {
        'x': nrm((BATCH, SEQ, D), 1.0),
        'mem': nrm((BATCH, MEM_LEN, D), 1.0),
        'mix_norm_w': gain((L, D)),
        'w_in': nrm((L, D, D_IN_PROJ), D ** -0.5),
        'da_q_norm_w': gain((L, DA_HEAD_DIM)),
        'da_k_norm_w': gain((L, DA_HEAD_DIM)),
        'da_lambda': nrm((L, 4, DA_HEAD_DIM), 0.1),
        'da_out_norm_w': gain((L, DA_V_DIM)),
        'gla_w_gate': nrm((L, GLA_GATE_RANK, GLA_HEADS * GLA_DK), GLA_GATE_RANK ** -0.5),
        'gla_b_gate': nrm((L, GLA_HEADS * GLA_DK), 0.01),
        'gla_out_norm_w': gain((L, GLA_DV)),
        'mem_norm_w': gain((L, D)),
        'w_mem_kv': nrm((L, D, 2 * xa_w), D ** -0.5),
        'xa_q_norm_w': gain((L, XA_HEAD_DIM)),
        'xa_k_norm_w': gain((L, XA_HEAD_DIM)),
        'w_br_da': nrm((L, da_w, D), da_w ** -0.5),
        'w_br_gla': nrm((L, gla_w, D), gla_w ** -0.5),
        'w_br_xa': nrm((L, xa_w, D), xa_w ** -0.5),
        'w_out': nrm((L, D, D), D ** -0.5),
        'ffn_norm_w': gain((L, D)),
        'peer_w_q': nrm((L, D, PEER_HEADS * PEER_QDIM), D ** -0.5),
        'peer_sub_keys': nrm((L, 2, PEER_N_KEYS, PEER_QDIM // 2), (PEER_QDIM // 2) ** -0.5),
        'peer_u': nrm((L, PEER_N_EXPERTS, D), D ** -0.5),
        'peer_v': nrm((L, PEER_N_EXPERTS, D), 0.1),
    }


def reference(x, mem, mix_norm_w, w_in, da_q_norm_w, da_k_norm_w, da_lambda, da_out_norm_w,
              gla_w_gate, gla_b_gate, gla_out_norm_w, mem_norm_w, w_mem_kv, xa_q_norm_w, xa_k_norm_w,
              w_br_da, w_br_gla, w_br_xa, w_out, ffn_norm_w, peer_w_q, peer_sub_keys, peer_u, peer_v):
    B, S, D = x.shape
    M = mem.shape[1]
    slopes = _alibi_slopes(DA_HEADS)
    splits = _in_proj_splits()
    for l in range(DEPTH):
        xn = _rmsnorm(x, mix_norm_w[l])
        proj = xn @ w_in[l]
        (da_q, da_k, da_v, gl_q, gl_k, gl_v, gl_r, gl_g, xa_q, gate_logits) = jnp.split(proj, splits, axis=-1)

        dq = _rmsnorm(da_q.reshape(B, S, DA_HEADS, 2, DA_HEAD_DIM), da_q_norm_w[l]).transpose(0, 2, 3, 1, 4)
        dk = _rmsnorm(da_k.reshape(B, S, DA_HEADS, 2, DA_HEAD_DIM), da_k_norm_w[l]).transpose(0, 2, 3, 1, 4)
        dv = da_v.reshape(B, S, DA_HEADS, DA_V_DIM).transpose(0, 2, 1, 3)
        lam_init = 0.8 - 0.6 * math.exp(-0.3 * l)
        lvec = da_lambda[l].astype(jnp.float32)
        lam = jnp.exp(jnp.sum(lvec[0] * lvec[1])) - jnp.exp(jnp.sum(lvec[2] * lvec[3])) + lam_init
        o_da = _diff_attention(dq, dk, dv, lam, slopes)
        o_da = (_rmsnorm(o_da, da_out_norm_w[l]) * (1.0 - lam_init)).transpose(0, 2, 1, 3).reshape(B, S, DA_HEADS * DA_V_DIM)

        log_a = jax.nn.log_sigmoid((gl_g @ gla_w_gate[l] + gla_b_gate[l]).astype(jnp.float32)) / GLA_TAU
        o_gla = _gla(gl_q.reshape(B, S, GLA_HEADS, GLA_DK), gl_k.reshape(B, S, GLA_HEADS, GLA_DK),
                     gl_v.reshape(B, S, GLA_HEADS, GLA_DV), log_a.reshape(B, S, GLA_HEADS, GLA_DK))
        o_gla = _rmsnorm(o_gla, gla_out_norm_w[l]).reshape(B, S, GLA_HEADS * GLA_DV) * jax.nn.silu(gl_r)

        mem_n = _rmsnorm(mem, mem_norm_w[l])
        mk, mv = jnp.split(mem_n @ w_mem_kv[l], 2, axis=-1)
        xq = _rmsnorm(xa_q.reshape(B, S, XA_HEADS, XA_HEAD_DIM), xa_q_norm_w[l])
        mk = _rmsnorm(mk.reshape(B, M, XA_HEADS, XA_HEAD_DIM), xa_k_norm_w[l])
        mv = mv.reshape(B, M, XA_HEADS, XA_HEAD_DIM)
        o_xa = _mem_attention(xq, mk, mv).reshape(B, S, XA_HEADS * XA_HEAD_DIM)

        g = jax.nn.sigmoid(gate_logits.astype(jnp.float32)).astype(x.dtype).reshape(B, S, N_BRANCH, D)
        merged = (g[:, :, 0] * (o_da @ w_br_da[l]) + g[:, :, 1] * (o_gla @ w_br_gla[l])
                  + g[:, :, 2] * (o_xa @ w_br_xa[l]))
        x = x + merged @ w_out[l]

        x = x + _peer(_rmsnorm(x, ffn_norm_w[l]), peer_w_q[l], peer_sub_keys[l], peer_u[l], peer_v[l])
    return x
```

```python
import functools
import math

import jax
import jax.numpy as jnp
import numpy as np
from jax import lax
from jax.experimental import pallas as pl
from jax.experimental.pallas import tpu as pltpu

F32 = jnp.float32
BF16 = jnp.bfloat16

LANES = 128
SUBLANES = 8
VMEM_LIMIT_BYTES = 56 * 1024 * 1024

D_MODEL = 1024
DA_HEADS = 4
DA_HEAD_DIM = 64
DA_V_DIM = 2 * DA_HEAD_DIM
GLA_HEADS = 4
GLA_DK = 64
GLA_DV = 128
GLA_GATE_RANK = 16
GLA_TAU = 16.0
XA_HEADS = 4
XA_HEAD_DIM = 128
N_BRANCH = 3
PEER_HEADS = 8
PEER_N_KEYS = 128
PEER_QDIM = 256
PEER_TOPK = 16
RMS_EPS = 1e-6

DA_W = DA_HEADS * 2 * DA_HEAD_DIM
GLA_QW = GLA_HEADS * GLA_DK
GLA_VW = GLA_HEADS * GLA_DV
XA_W = XA_HEADS * XA_HEAD_DIM
GATE_W = N_BRANCH * D_MODEL
PEER_SLOTS = PEER_HEADS * PEER_TOPK

C_DAQ = 0
C_DAK = C_DAQ + DA_W
C_DAV = C_DAK + DA_W
C_GLQ = C_DAV + DA_W
C_GLK = C_GLQ + GLA_QW
C_GLV = C_GLK + GLA_QW
C_GLR = C_GLV + GLA_VW
C_XAQ = C_GLR + GLA_VW
C_GATE = C_XAQ + XA_W
C_GLG = C_GATE + GATE_W
C_END = C_GLG + LANES

GLA_CHUNK = 128
GLA_LEVELS = 7

ROW_SUBLANES = 4
ROW_PIECES = 2 * ROW_SUBLANES


def _cparams(sem):
    return pltpu.CompilerParams(dimension_semantics=sem, vmem_limit_bytes=VMEM_LIMIT_BYTES)


def _resident(shape):
    nd = len(shape)
    return pl.BlockSpec(shape, lambda *_: (0,) * nd, pipeline_mode=pl.Buffered(1))


def _dot(a, b):
    return jnp.dot(a, b, preferred_element_type=F32)


def _dot_nt(a, b):
    return lax.dot_general(a, b, (((1,), (1,)), ((), ())), preferred_element_type=F32)


def _split2(t):
    hi = t.astype(BF16)
    lo = (t - hi.astype(F32)).astype(BF16)
    return hi, lo


def _group_sumsq(t, bd_ref):
    hi, lo = _split2(t * t)
    bd = bd_ref[...]
    return _dot(hi, bd) + _dot(lo, bd)


def _in_proj_body(x_ref, nw_ref, w_ref, wvt_ref, qw_ref, kw_ref, xw_ref, wg_ref, bg_ref, bd64_ref, bd128_ref,
                  daq_ref, dak_ref, dav_ref, glq_ref, glk_ref, glv_ref, glvt_ref, glr_ref, gla_ref, xaq_ref,
                  gate_ref):
    x = x_ref[...]
    ms = jnp.mean(x * x, axis=-1, keepdims=True)
    xn = (x * lax.rsqrt(ms + RMS_EPS) * nw_ref[...]).astype(BF16)

    def proj(c0, width):
        return _dot(xn, w_ref[:, c0:c0 + width])

    q = proj(C_DAQ, DA_W)
    daq_ref[...] = (q * lax.rsqrt(_group_sumsq(q, bd64_ref) * (1.0 / DA_HEAD_DIM) + RMS_EPS) * qw_ref[...]).astype(BF16)
    k = proj(C_DAK, DA_W)
    dak_ref[...] = (k * lax.rsqrt(_group_sumsq(k, bd64_ref) * (1.0 / DA_HEAD_DIM) + RMS_EPS) * kw_ref[...]).astype(BF16)
    dav_ref[...] = proj(C_DAV, DA_W).astype(BF16)

    glq_ref[...] = proj(C_GLQ, GLA_QW) * (GLA_DK ** -0.5)
    glk_ref[...] = proj(C_GLK, GLA_QW)
    glv_ref[...] = proj(C_GLV, GLA_VW).astype(BF16)
    glvt_ref[...] = _dot_nt(wvt_ref[...], xn).astype(BF16)
    r = proj(C_GLR, GLA_VW)
    glr_ref[...] = (r * jax.nn.sigmoid(r)).astype(BF16)

    g = proj(C_GLG, LANES)
    z = jnp.dot(g, wg_ref[...], preferred_element_type=F32, precision=lax.Precision.HIGHEST) + bg_ref[...]
    log_sig = jnp.minimum(z, 0.0) - jnp.log1p(jnp.exp(-jnp.abs(z)))
    gla_ref[...] = log_sig * (1.0 / GLA_TAU)

    xq = proj(C_XAQ, XA_W)
    xaq_ref[...] = (xq * lax.rsqrt(_group_sumsq(xq, bd128_ref) * (1.0 / XA_HEAD_DIM) + RMS_EPS) * xw_ref[...]).astype(BF16)

    for c in range(0, GATE_W, 512):
        gate_ref[:, c:c + 512] = jax.nn.sigmoid(proj(C_GATE + c, 512)).astype(BF16)


def _block_diag_ones(width, group):
    idx = np.arange(width) // group
    return jnp.asarray((idx[:, None] == idx[None, :]).astype(np.float32), dtype=BF16)


def _in_proj(x2, mix_norm_w, w_in, da_q_norm_w, da_k_norm_w, xa_q_norm_w, gla_w_gate, gla_b_gate, *, tm, interpret):
    n = x2.shape[0]
    c_glg_src = C_GLR + GLA_VW
    w = jnp.concatenate(
        [w_in[:, :c_glg_src], w_in[:, c_glg_src + GLA_GATE_RANK:], w_in[:, c_glg_src:c_glg_src + GLA_GATE_RANK],
         jnp.zeros((D_MODEL, LANES - GLA_GATE_RANK), w_in.dtype)], axis=1).astype(BF16)
    wvt = w_in[:, C_GLV:C_GLV + GLA_VW].T.astype(BF16)
    qw = jnp.tile(da_q_norm_w, 2 * DA_HEADS)[None, :] * (DA_HEAD_DIM ** -0.5)
    kw = jnp.tile(da_k_norm_w, 2 * DA_HEADS)[None, :]
    xw = jnp.tile(xa_q_norm_w, XA_HEADS)[None, :] * (XA_HEAD_DIM ** -0.5)
    wg = jnp.zeros((LANES, GLA_QW), F32).at[:GLA_GATE_RANK].set(gla_w_gate)
    bg = gla_b_gate[None, :]
    row = lambda width: pl.BlockSpec((tm, width), lambda i: (i, 0))
    out_widths = [(DA_W, BF16), (DA_W, BF16), (DA_W, BF16), (GLA_QW, F32), (GLA_QW, F32), (GLA_VW, BF16)]
    out_shape = [jax.ShapeDtypeStruct((n, wd), dt) for wd, dt in out_widths]
    out_specs = [row(wd) for wd, _ in out_widths]
    out_shape.append(jax.ShapeDtypeStruct((GLA_VW, n), BF16))
    out_specs.append(pl.BlockSpec((GLA_VW, tm), lambda i: (0, i)))
    for wd, dt in [(GLA_VW, BF16), (GLA_QW, F32), (XA_W, BF16), (GATE_W, BF16)]:
        out_shape.append(jax.ShapeDtypeStruct((n, wd), dt))
        out_specs.append(row(wd))
    return pl.pallas_call(
        _in_proj_body,
        grid=(n // tm,),
        in_specs=[row(D_MODEL), _resident((1, D_MODEL)), _resident((D_MODEL, C_END)), _resident((GLA_VW, D_MODEL)),
                  _resident((1, DA_W)), _resident((1, DA_W)), _resident((1, XA_W)), _resident((LANES, GLA_QW)),
                  _resident((1, GLA_QW)), _resident((DA_W, DA_W)), _resident((XA_W, XA_W))],
        out_specs=out_specs,
        out_shape=out_shape,
        compiler_params=_cparams(("parallel",)),
        name="in_proj",
        interpret=interpret,
    )(x2, mix_norm_w[None, :], w, wvt, qw, kw, xw, wg, bg, _block_diag_ones(DA_W, DA_HEAD_DIM),
      _block_diag_ones(XA_W, XA_HEAD_DIM))


def _diff_attn_body(qi_tab, kj_tab, q_ref, k_ref, v_ref, slope_ref, lam_ref, ow_ref, o_ref, m_scr, l_scr, acc_scr, *,
                    blk, lam_init):
    p = pl.program_id(2)
    qi = qi_tab[p]
    kj = kj_tab[p]

    @pl.when(kj == 0)
    def _():
        m_scr[...] = jnp.full(m_scr.shape, -jnp.inf, F32)
        l_scr[...] = jnp.zeros(l_scr.shape, F32)
        acc_scr[...] = jnp.zeros(acc_scr.shape, F32)

    q = q_ref[...]
    k = k_ref[...]
    v = v_ref[...]
    lane = lax.broadcasted_iota(jnp.int32, (1, LANES), 1)
    col = lax.broadcasted_iota(jnp.int32, (1, blk), 1)
    bias = slope_ref[:, :1] * (col + (kj - qi) * blk).astype(F32)

    def component(c, causal):
        qc = jnp.where((lane < DA_HEAD_DIM) == (c == 0), q, jnp.zeros_like(q))
        s = _dot_nt(qc, k) + bias
        if causal:
            row = lax.broadcasted_iota(jnp.int32, (blk, 1), 0)
            s = jnp.where(col <= row, s, -jnp.inf)
        m_prev = m_scr[c]
        m_new = jnp.maximum(m_prev, jnp.max(s, axis=-1, keepdims=True))
        alpha = jnp.exp(m_prev - m_new)
        pr = jnp.exp(s - m_new)
        l_scr[c] = alpha * l_scr[c] + jnp.sum(pr, axis=-1, keepdims=True)
        acc_scr[c] = alpha * acc_scr[c] + _dot(pr.astype(BF16), v)
        m_scr[c] = m_new

    @pl.when(kj < qi)
    def _():
        component(0, False)
        component(1, False)

    @pl.when(kj == qi)
    def _():
        component(0, True)
        component(1, True)
        lv = lam_ref[...]
        lam = (jnp.exp(jnp.sum(lv[0:1] * lv[1:2], axis=-1, keepdims=True))
               - jnp.exp(jnp.sum(lv[2:3] * lv[3:4], axis=-1, keepdims=True)) + lam_init)
        o = acc_scr[0] / l_scr[0] - lam * (acc_scr[1] / l_scr[1])
        ms = jnp.mean(o * o, axis=-1, keepdims=True)
        o_ref[...] = (o * lax.rsqrt(ms + RMS_EPS) * ow_ref[...] * (1.0 - lam_init)).astype(BF16)


def _diff_attn(daq, dak, dav, da_lambda, da_out_norm_w, *, batch, seq, blk, lam_init, interpret):
    nq = seq // blk
    pairs = [(qi, kj) for qi in range(nq) for kj in range(qi + 1)]
    qi_tab = jnp.asarray([p[0] for p in pairs], jnp.int32)
    kj_tab = jnp.asarray([p[1] for p in pairs], jnp.int32)
    slopes = jnp.exp2(-8.0 * jnp.arange(1, DA_HEADS + 1, dtype=F32) / DA_HEADS)
    slopes = jnp.broadcast_to(slopes[:, None, None], (DA_HEADS, 1, LANES))
    qmap = lambda b, h, p, qt, kt: (b * nq + qt[p], h)
    kmap = lambda b, h, p, qt, kt: (b * nq + kt[p], h)
    grid_spec = pltpu.PrefetchScalarGridSpec(
        num_scalar_prefetch=2,
        grid=(batch, DA_HEADS, len(pairs)),
        in_specs=[pl.BlockSpec((blk, DA_V_DIM), qmap), pl.BlockSpec((blk, DA_V_DIM), kmap),
                  pl.BlockSpec((blk, DA_V_DIM), kmap),
                  pl.BlockSpec((None, 1, LANES), lambda b, h, p, qt, kt: (h, 0, 0)),
                  pl.BlockSpec((4, DA_HEAD_DIM), lambda b, h, p, qt, kt: (0, 0)),
                  pl.BlockSpec((1, DA_V_DIM), lambda b, h, p, qt, kt: (0, 0))],
        out_specs=pl.BlockSpec((blk, DA_V_DIM), qmap),
        scratch_shapes=[pltpu.VMEM((2, blk, 1), F32), pltpu.VMEM((2, blk, 1), F32),
                        pltpu.VMEM((2, blk, DA_V_DIM), F32)],
    )
    return pl.pallas_call(
        functools.partial(_diff_attn_body, blk=blk, lam_init=lam_init),
        grid_spec=grid_spec,
        out_shape=jax.ShapeDtypeStruct((batch * seq, DA_HEADS * DA_V_DIM), BF16),
        compiler_params=_cparams(("parallel", "parallel", "arbitrary")),
        name="diff_attn",
        interpret=interpret,
    )(qi_tab, kj_tab, daq, dak, dav, slopes, da_lambda, da_out_norm_w[None, :])


def _gla_constants():
    c = GLA_CHUNK
    t = np.arange(c)[:, None]
    u = np.arange(c)[None, :]
    lmats, masks = [], []
    for lev in range(GLA_LEVELS):
        m = 1 << lev
        second = (t % (2 * m)) >= m
        first = ~second
        bnd = (t // m) * m
        lmats.append(second & (u >= bnd) & (u <= t))
    for lev in range(GLA_LEVELS):
        m = 1 << lev
        first = (t % (2 * m)) < m
        end = (t // m) * m + m - 1
        lmats.append(first & (u > t) & (u <= end))
    lmats.append(u <= t)
    lmats.append(u > t)
    for lev in range(GLA_LEVELS):
        m = 1 << lev
        s = np.arange(c)[None, :]
        masks.append(((t // (2 * m)) == (s // (2 * m))) & ((t % (2 * m)) >= m) & ((s % (2 * m)) < m))
    masks.append(t == np.arange(c)[None, :])
    lall = jnp.asarray(np.concatenate(lmats, axis=0).astype(np.float32), dtype=BF16)
    mall = jnp.asarray(np.stack(masks, axis=0).astype(np.float32))
    return lall, mall


def _gla_body(q_ref, k_ref, la_ref, v_ref, vt_ref, r_ref, lall_ref, mall_ref, ow_ref, o_ref, state_scr):
    c = GLA_CHUNK

    @pl.when(pl.program_id(1) == 0)
    def _():
        state_scr[...] = jnp.zeros(state_scr.shape, F32)

    q = q_ref[...]
    k = k_ref[...]
    g = la_ref[...]
    g1 = g.astype(BF16)
    r1 = g - g1.astype(F32)
    g2 = r1.astype(BF16)
    g3 = (r1 - g2.astype(F32)).astype(BF16)
    lall = lall_ref[...]
    e_all = jnp.exp(_dot(lall, g1) + _dot(lall, g2) + _dot(lall, g3))

    def rows(i):
        return e_all[i * c:(i + 1) * c]

    lane = lax.broadcasted_iota(jnp.int32, (1, GLA_QW), 1)
    head_masks = [(lane // GLA_DK == h).astype(F32) for h in range(GLA_HEADS)]
    a = [jnp.zeros((c, c), F32) for _ in range(GLA_HEADS)]
    for lev in range(GLA_LEVELS + 1):
        if lev < GLA_LEVELS:
            ql = q * rows(lev)
            kl = (k * rows(GLA_LEVELS + lev)).astype(BF16)
        else:
            ql = q
            kl = k.astype(BF16)
        mask = mall_ref[lev]
        for h in range(GLA_HEADS):
            a[h] = a[h] + mask * _dot_nt((ql * head_masks[h]).astype(BF16), kl)

    e_b = rows(2 * GLA_LEVELS)
    q_dec = q * e_b
    k_dec = (k * rows(2 * GLA_LEVELS + 1)).astype(BF16)
    chunk_decay = e_b[c - 1:c, :]
    for h in range(GLA_HEADS):
        vs = slice(h * GLA_DV, (h + 1) * GLA_DV)
        st = state_scr[h]
        o = _dot(a[h].astype(BF16), v_ref[:, vs]) + _dot_nt((q_dec * head_masks[h]).astype(BF16), st.astype(BF16))
        state_scr[h] = st * chunk_decay + _dot(vt_ref[vs, :], k_dec)
        ms = jnp.mean(o * o, axis=-1, keepdims=True)
        o_ref[:, vs] = (o * lax.rsqrt(ms + RMS_EPS) * ow_ref[...] * r_ref[:, vs].astype(F32)).astype(BF16)


def _gla(glq, glk, gla, glv, glvt, glr, gla_out_norm_w, *, batch, seq, interpret):
    c = GLA_CHUNK
    nc = seq // c
    lall, mall = _gla_constants()
    row = lambda width: pl.BlockSpec((c, width), lambda b, i: (b * nc + i, 0))
    return pl.pallas_call(
        _gla_body,
        grid=(batch, nc),
        in_specs=[row(GLA_QW), row(GLA_QW), row(GLA_QW), row(GLA_VW),
                  pl.BlockSpec((GLA_VW, c), lambda b, i: (0, b * nc + i)), row(GLA_VW),
                  _resident(lall.shape), _resident(mall.shape), _resident((1, GLA_DV))],
        out_specs=row(GLA_VW),
        out_shape=jax.ShapeDtypeStruct((batch * seq, GLA_VW), BF16),
        scratch_shapes=[pltpu.VMEM((GLA_HEADS, GLA_DV, GLA_QW), F32)],
        compiler_params=_cparams(("parallel", "arbitrary")),
        name="gla",
        interpret=interpret,
    )(glq, glk, gla, glv, glvt, glr, lall, mall, gla_out_norm_w[None, :])


def _mem_kv_body(mem_ref, nw_ref, w_ref, kw_ref, bd_ref, mk_ref, mv_ref):
    x = mem_ref[...]
    ms = jnp.mean(x * x, axis=-1, keepdims=True)
    xn = (x * lax.rsqrt(ms + RMS_EPS) * nw_ref[...]).astype(BF16)
    k = _dot(xn, w_ref[:, :XA_W])
    mk_ref[...] = (k * lax.rsqrt(_group_sumsq(k, bd_ref) * (1.0 / XA_HEAD_DIM) + RMS_EPS) * kw_ref[...]).astype(BF16)
    mv_ref[...] = _dot(xn, w_ref[:, XA_W:]).astype(BF16)


def _mem_kv(mem2, mem_norm_w, w_mem_kv, xa_k_norm_w, *, batch, mem_len, interpret):
    blk = pl.BlockSpec((mem_len, XA_W), lambda b: (b, 0))
    return pl.pallas_call(
        _mem_kv_body,
        grid=(batch,),
        in_specs=[pl.BlockSpec((mem_len, D_MODEL), lambda b: (b, 0)), _resident((1, D_MODEL)),
                  _resident((D_MODEL, 2 * XA_W)), _resident((1, XA_W)), _resident((XA_W, XA_W))],
        out_specs=[blk, blk],
        out_shape=[jax.ShapeDtypeStruct((batch * mem_len, XA_W), BF16)] * 2,
        compiler_params=_cparams(("parallel",)),
        name="mem_kv",
        interpret=interpret,
    )(mem2, mem_norm_w[None, :], w_mem_kv.astype(BF16), jnp.tile(xa_k_norm_w, XA_HEADS)[None, :],
      _block_diag_ones(XA_W, XA_HEAD_DIM))


def _piece_offset(j):
    return (j % 2) * (D_MODEL // 2) + (j // 2) * LANES


def _merge_body(x_ref, oda_ref, ogla_ref, xaq_ref, gate_ref, mk_ref, mv_ref, wda_ref, wgla_ref, wxa_ref, wout_ref,
                fw_ref, wq_ref, keys_ref, x1_ref, xnp_ref, sc_ref):
    br_xa = None
    for h in range(XA_HEADS):
        hs = slice(h * XA_HEAD_DIM, (h + 1) * XA_HEAD_DIM)
        s = _dot_nt(xaq_ref[:, hs], mk_ref[:, hs])
        s = s - jnp.max(s, axis=-1, keepdims=True)
        p = jnp.exp(s)
        p = p / jnp.sum(p, axis=-1, keepdims=True)
        o = _dot(p.astype(BF16), mv_ref[:, hs])
        t = _dot(o.astype(BF16), wxa_ref[hs, :])
        br_xa = t if br_xa is None else br_xa + t
    merged = (gate_ref[:, 0:D_MODEL].astype(F32) * _dot(oda_ref[...], wda_ref[...])
              + gate_ref[:, D_MODEL:2 * D_MODEL].astype(F32) * _dot(ogla_ref[...], wgla_ref[...])
              + gate_ref[:, 2 * D_MODEL:].astype(F32) * br_xa)
    x1 = x_ref[...] + _dot(merged.astype(BF16), wout_ref[...])
    x1_ref[...] = x1
    ms = jnp.mean(x1 * x1, axis=-1, keepdims=True)
    xn = x1 * lax.rsqrt(ms + RMS_EPS) * fw_ref[...]
    xnb = xn.astype(BF16)
    for j in range(ROW_PIECES):
        off = _piece_offset(j)
        xnp_ref[:, j * LANES:(j + 1) * LANES] = xn[:, off:off + LANES]
    pq = _dot(xnb, wq_ref[...])
    half = PEER_QDIM // 2
    for hp in range(2 * PEER_HEADS):
        q_hi, q_lo = _split2(pq[:, hp * half:(hp + 1) * half])
        keys = keys_ref[hp % 2]
        sc_ref[hp] = _dot_nt(keys, q_hi) + _dot_nt(keys, q_lo)


def _merge(x2, oda, ogla, xaq, gates, mk, mv, w_br_da, w_br_gla, w_br_xa, w_out, ffn_norm_w, peer_w_q, peer_sub_keys,
           *, seq, mem_len, tm, interpret):
    n = x2.shape[0]
    spb = seq // tm
    row = lambda width: pl.BlockSpec((tm, width), lambda i: (i, 0))
    mem_blk = pl.BlockSpec((mem_len, XA_W), lambda i: (i // spb, 0))
    nhp = 2 * PEER_HEADS
    return pl.pallas_call(
        _merge_body,
        grid=(n // tm,),
        in_specs=[row(D_MODEL), row(DA_W), row(GLA_VW), row(XA_W), row(GATE_W), mem_blk, mem_blk,
                  _resident((DA_W, D_MODEL)), _resident((GLA_VW, D_MODEL)), _resident((XA_W, D_MODEL)),
                  _resident((D_MODEL, D_MODEL)), _resident((1, D_MODEL)),
                  _resident((D_MODEL, PEER_HEADS * PEER_QDIM)), _resident((2, PEER_N_KEYS, PEER_QDIM // 2))],
        out_specs=[row(D_MODEL), row(D_MODEL), pl.BlockSpec((nhp, PEER_N_KEYS, tm), lambda i: (0, 0, i))],
        out_shape=[jax.ShapeDtypeStruct((n, D_MODEL), F32), jax.ShapeDtypeStruct((n, D_MODEL), F32),
                   jax.ShapeDtypeStruct((nhp, PEER_N_KEYS, n), F32)],
        compiler_params=_cparams(("parallel",)),
        name="merge",
        interpret=interpret,
    )(x2, oda, ogla, xaq, gates, mk, mv, w_br_da.astype(BF16), w_br_gla.astype(BF16), w_br_xa.astype(BF16),
      w_out.astype(BF16), ffn_norm_w[None, :], peer_w_q.astype(BF16), peer_sub_keys.astype(BF16))


def _top16(x, n_rows):
    iota = lax.broadcasted_iota(jnp.int32, x.shape, 0)
    vals, poss = [], []
    for _ in range(PEER_TOPK):
        m = jnp.max(x, axis=0, keepdims=True)
        pos = jnp.min(jnp.where(x == m, iota, n_rows), axis=0, keepdims=True)
        vals.append(m)
        poss.append(pos)
        x = jnp.where(iota == pos, -jnp.inf, x)
    return jnp.concatenate(vals, axis=0), jnp.concatenate(poss, axis=0)


def _select_rows(table, pos):
    out = jnp.zeros(pos.shape, table.dtype)
    for i in range(PEER_TOPK):
        out = jnp.where(pos == i, table[i:i + 1, :], out)
    return out


def _peer_topk_body(sc_ref, idx_ref, g_ref):
    for h in range(PEER_HEADS):
        s0, i0 = _top16(sc_ref[2 * h], PEER_N_KEYS)
        s1, i1 = _top16(sc_ref[2 * h + 1], PEER_N_KEYS)
        cand = jnp.concatenate([s0[i:i + 1, :] + s1 for i in range(PEER_TOPK)], axis=0)
        best, pos = _top16(cand, PEER_TOPK * PEER_TOPK)
        expert = (_select_rows(i0, jnp.right_shift(pos, 4)) * PEER_N_KEYS
                  + _select_rows(i1, jnp.bitwise_and(pos, PEER_TOPK - 1)))
        e = jnp.exp(best - best[0:1, :])
        hs = slice(h * PEER_TOPK, (h + 1) * PEER_TOPK)
        idx_ref[hs, :] = expert * ROW_SUBLANES
        g_ref[hs, :] = e / jnp.sum(e, axis=0, keepdims=True)


def _peer_topk(scores, *, tb, interpret):
    nhp, nk, n = scores.shape
    return pl.pallas_call(
        _peer_topk_body,
        grid=(n // tb,),
        in_specs=[pl.BlockSpec((nhp, nk, tb), lambda i: (0, 0, i))],
        out_specs=[pl.BlockSpec((PEER_SLOTS, tb), lambda i: (0, i))] * 2,
        out_shape=[jax.ShapeDtypeStruct((PEER_SLOTS, n), jnp.int32), jax.ShapeDtypeStruct((PEER_SLOTS, n), F32)],
        compiler_params=_cparams(("parallel",)),
        name="peer_topk",
        interpret=interpret,
    )(scores)


def _pack_table(t):
    e = t.shape[0]
    bits = lax.bitcast_convert_type(t.astype(BF16), jnp.uint16).astype(jnp.uint32)
    bits = bits.reshape(e, 2, ROW_SUBLANES, LANES)
    return (bits[:, 0] | (bits[:, 1] << 16)).reshape(e * ROW_SUBLANES, LANES)


def _gather_rows(idx_ref, tab_ref, g_scr, t, slot):
    for k in range(PEER_SLOTS):
        off = pl.multiple_of(idx_ref[t, k], ROW_SUBLANES)
        g_scr[slot, k * ROW_SUBLANES:(k + 1) * ROW_SUBLANES, :] = tab_ref[pl.ds(off, ROW_SUBLANES), :]


def _piece_mask():
    r = lax.broadcasted_iota(jnp.int32, (ROW_PIECES, PEER_SLOTS * ROW_PIECES), 0)
    c = lax.broadcasted_iota(jnp.int32, (ROW_PIECES, PEER_SLOTS * ROW_PIECES), 1)
    return (c % ROW_PIECES == r).astype(F32)


def _peer_u_body(idx_ref, x8_ref, g_ref, tab_ref, sel_ref, selt_ref, w_ref, g_scr, r_scr, *, tb):
    dmask = _piece_mask()

    def compute(t, slot):
        rows = pltpu.bitcast(g_scr[slot], BF16)
        x8 = x8_ref[pl.ds(pl.multiple_of(t * ROW_PIECES, ROW_PIECES), ROW_PIECES), :].astype(BF16)
        r = _dot_nt(x8, rows)
        r_scr[pl.ds(t, 1), :] = jnp.sum(r * dmask, axis=0, keepdims=True)

    _gather_rows(idx_ref, tab_ref, g_scr, 0, 0)

    def body(i, carry):
        t0 = 2 * i
        _gather_rows(idx_ref, tab_ref, g_scr, t0 + 1, 1)
        compute(t0, 0)
        _gather_rows(idx_ref, tab_ref, g_scr, jnp.minimum(t0 + 2, tb - 1), 0)
        compute(t0 + 1, 1)
        return carry

    lax.fori_loop(0, tb // 2, body, 0)
    hi, lo = _split2(r_scr[...])
    s = _dot(hi, sel_ref[...]) + _dot(lo, sel_ref[...])
    act = 0.5 * s * (1.0 + lax.erf(s * (2.0 ** -0.5)))
    w = (g_ref[...] * act).astype(BF16)
    w_ref[...] = _dot(w, selt_ref[...])


def _peer_v_body(idx_ref, w_ref, x1_ref, tab_ref, o_ref, g_scr, o8_scr, *, tb):
    dmask = _piece_mask()

    def compute(t, slot):
        rows = pltpu.bitcast(g_scr[slot], BF16)
        w8 = (w_ref[pl.ds(t, 1), :] * dmask).astype(BF16)
        o8_scr[pl.ds(pl.multiple_of(t * ROW_PIECES, ROW_PIECES), ROW_PIECES), :] = _dot(w8, rows)

    _gather_rows(idx_ref, tab_ref, g_scr, 0, 0)

    def body(i, carry):
        t0 = 2 * i
        _gather_rows(idx_ref, tab_ref, g_scr, t0 + 1, 1)
        compute(t0, 0)
        _gather_rows(idx_ref, tab_ref, g_scr, jnp.minimum(t0 + 2, tb - 1), 0)
        compute(t0 + 1, 1)
        return carry

    lax.fori_loop(0, tb // 2, body, 0)
    for j in range(ROW_PIECES):
        cs = slice(_piece_offset(j), _piece_offset(j) + LANES)
        o_ref[:, cs] = x1_ref[:, cs] + o8_scr[pl.ds(j, tb, stride=ROW_PIECES), :]


def _peer_u(idx, x8, gates, table, *, tb, interpret):
    n = idx.shape[0]
    sel_np = (np.arange(PEER_SLOTS * ROW_PIECES)[:, None] // ROW_PIECES == np.arange(PEER_SLOTS)[None, :])
    sel = jnp.asarray(sel_np.astype(np.float32), dtype=BF16)
    return pl.pallas_call(
        functools.partial(_peer_u_body, tb=tb),
        grid=(n // tb,),
        in_specs=[pl.BlockSpec((tb, PEER_SLOTS), lambda i: (i, 0), memory_space=pltpu.SMEM),
                  pl.BlockSpec((tb * ROW_PIECES, LANES), lambda i: (i, 0)),
                  pl.BlockSpec((tb, PEER_SLOTS), lambda i: (i, 0)),
                  _resident(table.shape), _resident(sel.shape), _resident(sel.T.shape)],
        out_specs=pl.BlockSpec((tb, PEER_SLOTS * ROW_PIECES), lambda i: (i, 0)),
        out_shape=jax.ShapeDtypeStruct((n, PEER_SLOTS * ROW_PIECES), F32),
        scratch_shapes=[pltpu.VMEM((2, PEER_SLOTS * ROW_SUBLANES, LANES), jnp.uint32),
                        pltpu.VMEM((tb, PEER_SLOTS * ROW_PIECES), F32)],
        compiler_params=_cparams(("parallel",)),
        name="peer_u",
        interpret=interpret,
    )(idx, x8, gates, table, sel, sel.T)


def _peer_v(idx, w_exp, x1, table, *, tb, interpret):
    n = idx.shape[0]
    return pl.pallas_call(
        functools.partial(_peer_v_body, tb=tb),
        grid=(n // tb,),
        in_specs=[pl.BlockSpec((tb, PEER_SLOTS), lambda i: (i, 0), memory_space=pltpu.SMEM),
                  pl.BlockSpec((tb, PEER_SLOTS * ROW_PIECES), lambda i: (i, 0)),
                  pl.BlockSpec((tb, D_MODEL), lambda i: (i, 0)),
                  _resident(table.shape)],
        out_specs=pl.BlockSpec((tb, D_MODEL), lambda i: (i, 0)),
        out_shape=jax.ShapeDtypeStruct((n, D_MODEL), F32),
        scratch_shapes=[pltpu.VMEM((2, PEER_SLOTS * ROW_SUBLANES, LANES), jnp.uint32),
                        pltpu.VMEM((tb * ROW_PIECES, LANES), F32)],
        compiler_params=_cparams(("parallel",)),
        name="peer_v",
        interpret=interpret,
    )(idx, w_exp, x1, table)


def _layer(x2, mem2, p, *, layer, batch, seq, mem_len, tm, attn_blk, peer_tb, interpret):
    lam_init = 0.8 - 0.6 * math.exp(-0.3 * layer)
    (daq, dak, dav, glq, glk, glv, glvt, glr, gla, xaq, gates) = _in_proj(
        x2, p["mix_norm_w"], p["w_in"], p["da_q_norm_w"], p["da_k_norm_w"], p["xa_q_norm_w"], p["gla_w_gate"],
        p["gla_b_gate"], tm=tm, interpret=interpret)
    oda = _diff_attn(daq, dak, dav, p["da_lambda"], p["da_out_norm_w"], batch=batch, seq=seq, blk=attn_blk,
                     lam_init=lam_init, interpret=interpret)
    ogla = _gla(glq, glk, gla, glv, glvt, glr, p["gla_out_norm_w"], batch=batch, seq=seq, interpret=interpret)
    mk, mv = _mem_kv(mem2, p["mem_norm_w"], p["w_mem_kv"], p["xa_k_norm_w"], batch=batch, mem_len=mem_len,
                     interpret=interpret)
    x1, xnp, scores = _merge(x2, oda, ogla, xaq, gates, mk, mv, p["w_br_da"], p["w_br_gla"], p["w_br_xa"], p["w_out"],
                             p["ffn_norm_w"], p["peer_w_q"], p["peer_sub_keys"], seq=seq, mem_len=mem_len, tm=tm,
                             interpret=interpret)
    idx_t, g_t = _peer_topk(scores, tb=peer_tb, interpret=interpret)
    idx = idx_t.T
    n = x2.shape[0]
    w_exp = _peer_u(idx, xnp.reshape(n * ROW_PIECES, LANES), g_t.T, _pack_table(p["peer_u"]), tb=peer_tb,
                    interpret=interpret)
    return _peer_v(idx, w_exp, x1, _pack_table(p["peer_v"]), tb=peer_tb, interpret=interpret)


_PARAM_NAMES = ("mix_norm_w", "w_in", "da_q_norm_w", "da_k_norm_w", "da_lambda", "da_out_norm_w", "gla_w_gate",
                "gla_b_gate", "gla_out_norm_w", "mem_norm_w", "w_mem_kv", "xa_q_norm_w", "xa_k_norm_w", "w_br_da",
                "w_br_gla", "w_br_xa", "w_out", "ffn_norm_w", "peer_w_q", "peer_sub_keys", "peer_u", "peer_v")


def _forward(x, mem, params, *, tm=256, attn_blk=512, peer_tb=128, interpret=False):
    batch, seq, d = x.shape
    mem_len = mem.shape[1]
    x2 = x.reshape(batch * seq, d)
    mem2 = mem.reshape(batch * mem_len, d)
    depth = params["w_in"].shape[0]
    for layer in range(depth):
        p = {name: params[name][layer] for name in _PARAM_NAMES}
        x2 = _layer(x2, mem2, p, layer=layer, batch=batch, seq=seq, mem_len=mem_len, tm=tm, attn_blk=attn_blk,
                    peer_tb=peer_tb, interpret=interpret)
    return x2.reshape(batch, seq, d)


def kernel(x, mem, mix_norm_w, w_in, da_q_norm_w, da_k_norm_w, da_lambda, da_out_norm_w, gla_w_gate, gla_b_gate,
           gla_out_norm_w, mem_norm_w, w_mem_kv, xa_q_norm_w, xa_k_norm_w, w_br_da, w_br_gla, w_br_xa, w_out,
           ffn_norm_w, peer_w_q, peer_sub_keys, peer_u, peer_v):
    params = dict(zip(_PARAM_NAMES, (mix_norm_w, w_in, da_q_norm_w, da_k_norm_w, da_lambda, da_out_norm_w, gla_w_gate,
                                     gla_b_gate, gla_out_norm_w, mem_norm_w, w_mem_kv, xa_q_norm_w, xa_k_norm_w,
                                     w_br_da, w_br_gla, w_br_xa, w_out, ffn_norm_w, peer_w_q, peer_sub_keys, peer_u,
                                     peer_v)))
    return _forward(x, mem, params)
```

```python
import functools
import math

import jax
import jax.numpy as jnp
import numpy as np
from jax import lax
from jax.experimental import pallas as pl
from jax.experimental.pallas import tpu as pltpu

F32 = jnp.float32
BF16 = jnp.bfloat16

LANES = 128
SUBLANES = 8
VMEM_LIMIT_BYTES = 56 * 1024 * 1024

D_MODEL = 1024
DA_HEADS = 4
DA_HEAD_DIM = 64
DA_V_DIM = 2 * DA_HEAD_DIM
GLA_HEADS = 4
GLA_DK = 64
GLA_DV = 128
GLA_GATE_RANK = 16
GLA_TAU = 16.0
XA_HEADS = 4
XA_HEAD_DIM = 128
N_BRANCH = 3
PEER_HEADS = 8
PEER_N_KEYS = 128
PEER_QDIM = 256
PEER_TOPK = 16
RMS_EPS = 1e-6
LOG2E = math.log2(math.e)

DA_W = DA_HEADS * 2 * DA_HEAD_DIM
GLA_QW = GLA_HEADS * GLA_DK
GLA_VW = GLA_HEADS * GLA_DV
XA_W = XA_HEADS * XA_HEAD_DIM
GATE_W = N_BRANCH * D_MODEL
PEER_SLOTS = PEER_HEADS * PEER_TOPK

C_DAQ = 0
C_DAK = C_DAQ + DA_W
C_DAV = C_DAK + DA_W
C_GLQ = C_DAV + DA_W
C_GLK = C_GLQ + GLA_QW
C_GLV = C_GLK + GLA_QW
C_GLR = C_GLV + GLA_VW
C_XAQ = C_GLR + GLA_VW
C_GATE = C_XAQ + XA_W
C_GLG = C_GATE + GATE_W
C_END = C_GLG + LANES

GLA_CHUNK = 128
GLA_LEVELS = 7

ROW_SUBLANES = 4
ROW_PIECES = 2 * ROW_SUBLANES


def _cparams(sem):
    return pltpu.CompilerParams(dimension_semantics=sem, vmem_limit_bytes=VMEM_LIMIT_BYTES)


def _resident(shape):
    nd = len(shape)
    return pl.BlockSpec(shape, lambda *_: (0,) * nd, pipeline_mode=pl.Buffered(1))


def _dot(a, b):
    return jnp.dot(a, b, preferred_element_type=F32)


def _dot_nt(a, b):
    return lax.dot_general(a, b, (((1,), (1,)), ((), ())), preferred_element_type=F32)


def _split2(t):
    hi = t.astype(BF16)
    lo = (t - hi.astype(F32)).astype(BF16)
    return hi, lo


def _group_sumsq(t, bd_ref):
    hi, lo = _split2(t * t)
    bd = bd_ref[...]
    return _dot(hi, bd) + _dot(lo, bd)


def _in_proj_body(x_ref, nw_ref, w_ref, wvt_ref, qw_ref, kw_ref, xw_ref, wg_ref, bg_ref, bd64_ref, bd128_ref,
                  daq_ref, dak_ref, dav_ref, glq_ref, glk_ref, glv_ref, glvt_ref, glr_ref, gla_ref, xaq_ref,
                  gate_ref, *, seq):
    x = x_ref[...]
    ms = jnp.mean(x * x, axis=-1, keepdims=True)
    xn = (x * lax.rsqrt(ms + RMS_EPS) * nw_ref[...]).astype(BF16)

    def proj(c0, width):
        return _dot(xn, w_ref[:, c0:c0 + width])

    q = proj(C_DAQ, DA_W)
    q = q * lax.rsqrt(_group_sumsq(q, bd64_ref) * (1.0 / DA_HEAD_DIM) + RMS_EPS) * qw_ref[...]
    k = proj(C_DAK, DA_W)
    k = k * lax.rsqrt(_group_sumsq(k, bd64_ref) * (1.0 / DA_HEAD_DIM) + RMS_EPS) * kw_ref[...]
    tm = x.shape[0]
    lane = lax.broadcasted_iota(jnp.int32, (1, LANES), 1)
    pos = (pl.program_id(0) * tm + lax.broadcasted_iota(jnp.int32, (tm, 1), 0)) % seq
    pos = pos.astype(F32)
    for h in range(DA_HEADS):
        hs = slice(h * DA_V_DIM, (h + 1) * DA_V_DIM)
        b0 = pos * (LOG2E * 2.0 ** (-8.0 * (h + 1) / DA_HEADS))
        p0 = b0.astype(BF16).astype(F32)
        b1 = b0 - p0
        p1 = b1.astype(BF16).astype(F32)
        p2 = b1 - p1
        for c in range(2):
            own = (lane < DA_HEAD_DIM) if c == 0 else (lane >= DA_HEAD_DIM)
            first = DA_HEAD_DIM * (1 - c)
            cs = slice((2 * h + c) * LANES, (2 * h + c + 1) * LANES)
            ones = ((lane >= first) & (lane < first + 3)).astype(F32)
            daq_ref[:, cs] = jnp.where(own, q[:, hs], ones).astype(BF16)
            feat = jnp.where(lane == first, p0, jnp.where(lane == first + 1, p1, jnp.where(lane == first + 2, p2, 0.0)))
            dak_ref[:, cs] = jnp.where(own, k[:, hs], feat).astype(BF16)
    dav_ref[...] = proj(C_DAV, DA_W).astype(BF16)

    glq_ref[...] = proj(C_GLQ, GLA_QW) * (GLA_DK ** -0.5)
    glk_ref[...] = proj(C_GLK, GLA_QW)
    glv_ref[...] = proj(C_GLV, GLA_VW).astype(BF16)
    glvt_ref[...] = _dot_nt(wvt_ref[...], xn).astype(BF16)
    r = proj(C_GLR, GLA_VW)
    glr_ref[...] = (r * jax.nn.sigmoid(r)).astype(BF16)

    g = proj(C_GLG, LANES)
    z = jnp.dot(g, wg_ref[...], preferred_element_type=F32, precision=lax.Precision.HIGHEST) + bg_ref[...]
    log_sig = jnp.minimum(z, 0.0) - jnp.log1p(jnp.exp(-jnp.abs(z)))
    gla_ref[...] = log_sig * (1.0 / GLA_TAU)

    xq = proj(C_XAQ, XA_W)
    xaq_ref[...] = (xq * lax.rsqrt(_group_sumsq(xq, bd128_ref) * (1.0 / XA_HEAD_DIM) + RMS_EPS) * xw_ref[...]).astype(BF16)

    for c in range(0, GATE_W, 512):
        gate_ref[:, c:c + 512] = jax.nn.sigmoid(proj(C_GATE + c, 512)).astype(BF16)


def _block_diag_ones(width, group):
    idx = np.arange(width) // group
    return jnp.asarray((idx[:, None] == idx[None, :]).astype(np.float32), dtype=BF16)


def _in_proj(x2, mix_norm_w, w_in, da_q_norm_w, da_k_norm_w, xa_q_norm_w, gla_w_gate, gla_b_gate, *, seq, tm,
             interpret):
    n = x2.shape[0]
    c_glg_src = C_GLR + GLA_VW
    w = jnp.concatenate(
        [w_in[:, :c_glg_src], w_in[:, c_glg_src + GLA_GATE_RANK:], w_in[:, c_glg_src:c_glg_src + GLA_GATE_RANK],
         jnp.zeros((D_MODEL, LANES - GLA_GATE_RANK), w_in.dtype)], axis=1).astype(BF16)
    wvt = w_in[:, C_GLV:C_GLV + GLA_VW].T.astype(BF16)
    qw = jnp.tile(da_q_norm_w, 2 * DA_HEADS)[None, :] * (DA_HEAD_DIM ** -0.5 * LOG2E)
    kw = jnp.tile(da_k_norm_w, 2 * DA_HEADS)[None, :]
    xw = jnp.tile(xa_q_norm_w, XA_HEADS)[None, :] * (XA_HEAD_DIM ** -0.5)
    wg = jnp.zeros((LANES, GLA_QW), F32).at[:GLA_GATE_RANK].set(gla_w_gate)
    bg = gla_b_gate[None, :]
    row = lambda width: pl.BlockSpec((tm, width), lambda i: (i, 0))
    out_widths = [(2 * DA_W, BF16), (2 * DA_W, BF16), (DA_W, BF16), (GLA_QW, F32), (GLA_QW, F32), (GLA_VW, BF16)]
    out_shape = [jax.ShapeDtypeStruct((n, wd), dt) for wd, dt in out_widths]
    out_specs = [row(wd) for wd, _ in out_widths]
    out_shape.append(jax.ShapeDtypeStruct((GLA_VW, n), BF16))
    out_specs.append(pl.BlockSpec((GLA_VW, tm), lambda i: (0, i)))
    for wd, dt in [(GLA_VW, BF16), (GLA_QW, F32), (XA_W, BF16), (GATE_W, BF16)]:
        out_shape.append(jax.ShapeDtypeStruct((n, wd), dt))
        out_specs.append(row(wd))
    return pl.pallas_call(
        functools.partial(_in_proj_body, seq=seq),
        grid=(n // tm,),
        in_specs=[row(D_MODEL), _resident((1, D_MODEL)), _resident((D_MODEL, C_END)), _resident((GLA_VW, D_MODEL)),
                  _resident((1, DA_W)), _resident((1, DA_W)), _resident((1, XA_W)), _resident((LANES, GLA_QW)),
                  _resident((1, GLA_QW)), _resident((DA_W, DA_W)), _resident((XA_W, XA_W))],
        out_specs=out_specs,
        out_shape=out_shape,
        compiler_params=_cparams(("parallel",)),
        name="in_proj",
        interpret=interpret,
    )(x2, mix_norm_w[None, :], w, wvt, qw, kw, xw, wg, bg, _block_diag_ones(DA_W, DA_HEAD_DIM),
      _block_diag_ones(XA_W, XA_HEAD_DIM))


def _diff_attn_body(qi_tab, kj_tab, q_ref, k_ref, v_ref, lam_ref, ow_ref, o_ref, m_scr, acc_scr, *, blk_q, blk_k,
                    lam_init):
    p = pl.program_id(2)
    qi = qi_tab[p]
    kj = kj_tab[p]
    last = (qi * blk_q) // blk_k

    @pl.when(kj == 0)
    def _():
        m_scr[...] = jnp.full(m_scr.shape, -jnp.inf, F32)
        acc_scr[...] = jnp.zeros(acc_scr.shape, F32)

    v = v_ref[...]
    v_ones = jnp.concatenate([v, jnp.ones_like(v)], axis=1)

    def component(c, causal):
        cs = slice(c * LANES, (c + 1) * LANES)
        s = _dot_nt(q_ref[:, cs], k_ref[:, cs])
        if causal:
            row = lax.broadcasted_iota(jnp.int32, (blk_q, 1), 0) + qi * blk_q
            col = lax.broadcasted_iota(jnp.int32, (1, blk_k), 1) + kj * blk_k
            s = jnp.where(col <= row, s, -jnp.inf)
        m_prev = m_scr[c]
        m_new = jnp.maximum(m_prev, jnp.max(s, axis=-1, keepdims=True))
        acc_scr[c] = jnp.exp2(m_prev - m_new) * acc_scr[c] + _dot(jnp.exp2(s - m_new).astype(BF16), v_ones)
        m_scr[c] = m_new

    @pl.when(kj < last)
    def _():
        component(0, False)
        component(1, False)

    @pl.when(kj == last)
    def _():
        component(0, True)
        component(1, True)
        lv = lam_ref[...]
        lam = (jnp.exp(jnp.sum(lv[0:1] * lv[1:2], axis=-1, keepdims=True))
               - jnp.exp(jnp.sum(lv[2:3] * lv[3:4], axis=-1, keepdims=True)) + lam_init)
        a0 = acc_scr[0]
        a1 = acc_scr[1]
        o = a0[:, :DA_V_DIM] / a0[:, DA_V_DIM:] - lam * (a1[:, :DA_V_DIM] / a1[:, DA_V_DIM:])
        ms = jnp.mean(o * o, axis=-1, keepdims=True)
        o_ref[...] = (o * lax.rsqrt(ms + RMS_EPS) * ow_ref[...] * (1.0 - lam_init)).astype(BF16)


def _diff_attn(daq, dak, dav, da_lambda, da_out_norm_w, *, batch, seq, blk_q, blk_k, lam_init, interpret):
    assert blk_k % blk_q == 0 and seq % blk_k == 0
    nq = seq // blk_q
    nk = seq // blk_k
    pairs = [(qi, kj) for qi in range(nq) for kj in range((qi * blk_q) // blk_k + 1)]
    qi_tab = jnp.asarray([p[0] for p in pairs], jnp.int32)
    kj_tab = jnp.asarray([p[1] for p in pairs], jnp.int32)
    qmap = lambda b, h, p, qt, kt: (b * nq + qt[p], h)
    kmap = lambda b, h, p, qt, kt: (b * nk + kt[p], h)
    grid_spec = pltpu.PrefetchScalarGridSpec(
        num_scalar_prefetch=2,
        grid=(batch, DA_HEADS, len(pairs)),
        in_specs=[pl.BlockSpec((blk_q, 2 * LANES), qmap), pl.BlockSpec((blk_k, 2 * LANES), kmap),
                  pl.BlockSpec((blk_k, DA_V_DIM), kmap),
                  pl.BlockSpec((4, DA_HEAD_DIM), lambda b, h, p, qt, kt: (0, 0)),
                  pl.BlockSpec((1, DA_V_DIM), lambda b, h, p, qt, kt: (0, 0))],
        out_specs=pl.BlockSpec((blk_q, DA_V_DIM), qmap),
        scratch_shapes=[pltpu.VMEM((2, blk_q, 1), F32), pltpu.VMEM((2, blk_q, 2 * DA_V_DIM), F32)],
    )
    return pl.pallas_call(
        functools.partial(_diff_attn_body, blk_q=blk_q, blk_k=blk_k, lam_init=lam_init),
        grid_spec=grid_spec,
        out_shape=jax.ShapeDtypeStruct((batch * seq, DA_HEADS * DA_V_DIM), BF16),
        compiler_params=_cparams(("parallel", "parallel", "arbitrary")),
        name="diff_attn",
        interpret=interpret,
    )(qi_tab, kj_tab, daq, dak, dav, da_lambda, da_out_norm_w[None, :])


def _gla_constants():
    c = GLA_CHUNK
    t = np.arange(c)[:, None]
    u = np.arange(c)[None, :]
    lmats, masks = [], []
    for lev in range(GLA_LEVELS):
        m = 1 << lev
        second = (t % (2 * m)) >= m
        first = ~second
        bnd = (t // m) * m
        lmats.append(second & (u >= bnd) & (u <= t))
    for lev in range(GLA_LEVELS):
        m = 1 << lev
        first = (t % (2 * m)) < m
        end = (t // m) * m + m - 1
        lmats.append(first & (u > t) & (u <= end))
    lmats.append(u <= t)
    lmats.append(u > t)
    for lev in range(GLA_LEVELS):
        m = 1 << lev
        s = np.arange(c)[None, :]
        masks.append(((t // (2 * m)) == (s // (2 * m))) & ((t % (2 * m)) >= m) & ((s % (2 * m)) < m))
    masks.append(t == np.arange(c)[None, :])
    lall = jnp.asarray(np.concatenate(lmats, axis=0).astype(np.float32), dtype=BF16)
    mall = jnp.asarray(np.stack(masks, axis=0).astype(np.float32))
    return lall, mall


def _gla_body(q_ref, k_ref, la_ref, v_ref, vt_ref, r_ref, lall_ref, mall_ref, ow_ref, o_ref, state_scr):
    c = GLA_CHUNK

    @pl.when(pl.program_id(1) == 0)
    def _():
        state_scr[...] = jnp.zeros(state_scr.shape, F32)

    q = q_ref[...]
    k = k_ref[...]
    g = la_ref[...]
    g1 = g.astype(BF16)
    r1 = g - g1.astype(F32)
    g2 = r1.astype(BF16)
    g3 = (r1 - g2.astype(F32)).astype(BF16)
    lall = lall_ref[...]
    e_all = jnp.exp(_dot(lall, g1) + _dot(lall, g2) + _dot(lall, g3))

    def rows(i):
        return e_all[i * c:(i + 1) * c]

    lane = lax.broadcasted_iota(jnp.int32, (1, GLA_QW), 1)
    head_masks = [(lane // GLA_DK == h).astype(F32) for h in range(GLA_HEADS)]
    a = [jnp.zeros((c, c), F32) for _ in range(GLA_HEADS)]
    for lev in range(GLA_LEVELS + 1):
        if lev < GLA_LEVELS:
            ql = q * rows(lev)
            kl = (k * rows(GLA_LEVELS + lev)).astype(BF16)
        else:
            ql = q
            kl = k.astype(BF16)
        mask = mall_ref[lev]
        for h in range(GLA_HEADS):
            a[h] = a[h] + mask * _dot_nt((ql * head_masks[h]).astype(BF16), kl)

    e_b = rows(2 * GLA_LEVELS)
    q_dec = q * e_b
    k_dec = (k * rows(2 * GLA_LEVELS + 1)).astype(BF16)
    chunk_decay = e_b[c - 1:c, :]
    for h in range(GLA_HEADS):
        vs = slice(h * GLA_DV, (h + 1) * GLA_DV)
        st = state_scr[h]
        o = _dot(a[h].astype(BF16), v_ref[:, vs]) + _dot_nt((q_dec * head_masks[h]).astype(BF16), st.astype(BF16))
        state_scr[h] = st * chunk_decay + _dot(vt_ref[vs, :], k_dec)
        ms = jnp.mean(o * o, axis=-1, keepdims=True)
        o_ref[:, vs] = (o * lax.rsqrt(ms + RMS_EPS) * ow_ref[...] * r_ref[:, vs].astype(F32)).astype(BF16)


def _gla(glq, glk, gla, glv, glvt, glr, gla_out_norm_w, *, batch, seq, interpret):
    c = GLA_CHUNK
    nc = seq // c
    lall, mall = _gla_constants()
    row = lambda width: pl.BlockSpec((c, width), lambda b, i: (b * nc + i, 0))
    return pl.pallas_call(
        _gla_body,
        grid=(batch, nc),
        in_specs=[row(GLA_QW), row(GLA_QW), row(GLA_QW), row(GLA_VW),
                  pl.BlockSpec((GLA_VW, c), lambda b, i: (0, b * nc + i)), row(GLA_VW),
                  _resident(lall.shape), _resident(mall.shape), _resident((1, GLA_DV))],
        out_specs=row(GLA_VW),
        out_shape=jax.ShapeDtypeStruct((batch * seq, GLA_VW), BF16),
        scratch_shapes=[pltpu.VMEM((GLA_HEADS, GLA_DV, GLA_QW), F32)],
        compiler_params=_cparams(("parallel", "arbitrary")),
        name="gla",
        interpret=interpret,
    )(glq, glk, gla, glv, glvt, glr, lall, mall, gla_out_norm_w[None, :])


def _mem_kv_body(mem_ref, nw_ref, w_ref, kw_ref, bd_ref, mk_ref, mv_ref):
    x = mem_ref[...]
    ms = jnp.mean(x * x, axis=-1, keepdims=True)
    xn = (x * lax.rsqrt(ms + RMS_EPS) * nw_ref[...]).astype(BF16)
    k = _dot(xn, w_ref[:, :XA_W])
    mk_ref[...] = (k * lax.rsqrt(_group_sumsq(k, bd_ref) * (1.0 / XA_HEAD_DIM) + RMS_EPS) * kw_ref[...]).astype(BF16)
    mv_ref[...] = _dot(xn, w_ref[:, XA_W:]).astype(BF16)


def _mem_kv(mem2, mem_norm_w, w_mem_kv, xa_k_norm_w, *, batch, mem_len, interpret):
    blk = pl.BlockSpec((mem_len, XA_W), lambda b: (b, 0))
    return pl.pallas_call(
        _mem_kv_body,
        grid=(batch,),
        in_specs=[pl.BlockSpec((mem_len, D_MODEL), lambda b: (b, 0)), _resident((1, D_MODEL)),
                  _resident((D_MODEL, 2 * XA_W)), _resident((1, XA_W)), _resident((XA_W, XA_W))],
        out_specs=[blk, blk],
        out_shape=[jax.ShapeDtypeStruct((batch * mem_len, XA_W), BF16)] * 2,
        compiler_params=_cparams(("parallel",)),
        name="mem_kv",
        interpret=interpret,
    )(mem2, mem_norm_w[None, :], w_mem_kv.astype(BF16), jnp.tile(xa_k_norm_w, XA_HEADS)[None, :],
      _block_diag_ones(XA_W, XA_HEAD_DIM))


def _piece_offset(j):
    return (j % 2) * (D_MODEL // 2) + (j // 2) * LANES


def _merge_body(x_ref, oda_ref, ogla_ref, xaq_ref, gate_ref, mk_ref, mv_ref, wda_ref, wgla_ref, wxa_ref, wout_ref,
                fw_ref, wq_ref, keys_ref, x1_ref, xnp_ref, sc_ref):
    br_xa = None
    for h in range(XA_HEADS):
        hs = slice(h * XA_HEAD_DIM, (h + 1) * XA_HEAD_DIM)
        s = _dot_nt(xaq_ref[:, hs], mk_ref[:, hs])
        s = s - jnp.max(s, axis=-1, keepdims=True)
        p = jnp.exp(s)
        p = p / jnp.sum(p, axis=-1, keepdims=True)
        o = _dot(p.astype(BF16), mv_ref[:, hs])
        t = _dot(o.astype(BF16), wxa_ref[hs, :])
        br_xa = t if br_xa is None else br_xa + t
    merged = (gate_ref[:, 0:D_MODEL].astype(F32) * _dot(oda_ref[...], wda_ref[...])
              + gate_ref[:, D_MODEL:2 * D_MODEL].astype(F32) * _dot(ogla_ref[...], wgla_ref[...])
              + gate_ref[:, 2 * D_MODEL:].astype(F32) * br_xa)
    x1 = x_ref[...] + _dot(merged.astype(BF16), wout_ref[...])
    x1_ref[...] = x1
    ms = jnp.mean(x1 * x1, axis=-1, keepdims=True)
    xn = x1 * lax.rsqrt(ms + RMS_EPS) * fw_ref[...]
    xnb = xn.astype(BF16)
    for j in range(ROW_PIECES):
        off = _piece_offset(j)
        xnp_ref[:, j * LANES:(j + 1) * LANES] = xn[:, off:off + LANES]
    pq = _dot(xnb, wq_ref[...])
    half = PEER_QDIM // 2
    for hp in range(2 * PEER_HEADS):
        q_hi, q_lo = _split2(pq[:, hp * half:(hp + 1) * half])
        keys = keys_ref[hp % 2]
        sc_ref[hp] = _dot_nt(keys, q_hi) + _dot_nt(keys, q_lo)


def _merge(x2, oda, ogla, xaq, gates, mk, mv, w_br_da, w_br_gla, w_br_xa, w_out, ffn_norm_w, peer_w_q, peer_sub_keys,
           *, seq, mem_len, tm, interpret):
    n = x2.shape[0]
    spb = seq // tm
    row = lambda width: pl.BlockSpec((tm, width), lambda i: (i, 0))
    mem_blk = pl.BlockSpec((mem_len, XA_W), lambda i: (i // spb, 0))
    nhp = 2 * PEER_HEADS
    return pl.pallas_call(
        _merge_body,
        grid=(n // tm,),
        in_specs=[row(D_MODEL), row(DA_W), row(GLA_VW), row(XA_W), row(GATE_W), mem_blk, mem_blk,
                  _resident((DA_W, D_MODEL)), _resident((GLA_VW, D_MODEL)), _resident((XA_W, D_MODEL)),
                  _resident((D_MODEL, D_MODEL)), _resident((1, D_MODEL)),
                  _resident((D_MODEL, PEER_HEADS * PEER_QDIM)), _resident((2, PEER_N_KEYS, PEER_QDIM // 2))],
        out_specs=[row(D_MODEL), row(D_MODEL), pl.BlockSpec((nhp, PEER_N_KEYS, tm), lambda i: (0, 0, i))],
        out_shape=[jax.ShapeDtypeStruct((n, D_MODEL), F32), jax.ShapeDtypeStruct((n, D_MODEL), F32),
                   jax.ShapeDtypeStruct((nhp, PEER_N_KEYS, n), F32)],
        compiler_params=_cparams(("parallel",)),
        name="merge",
        interpret=interpret,
    )(x2, oda, ogla, xaq, gates, mk, mv, w_br_da.astype(BF16), w_br_gla.astype(BF16), w_br_xa.astype(BF16),
      w_out.astype(BF16), ffn_norm_w[None, :], peer_w_q.astype(BF16), peer_sub_keys.astype(BF16))


def _top16(x, n_rows):
    iota = lax.broadcasted_iota(jnp.int32, x.shape, 0)
    vals, poss = [], []
    for _ in range(PEER_TOPK):
        m = jnp.max(x, axis=0, keepdims=True)
        pos = jnp.min(jnp.where(x == m, iota, n_rows), axis=0, keepdims=True)
        vals.append(m)
        poss.append(pos)
        x = jnp.where(iota == pos, -jnp.inf, x)
    return jnp.concatenate(vals, axis=0), jnp.concatenate(poss, axis=0)


def _select_rows(table, pos):
    out = jnp.zeros(pos.shape, table.dtype)
    for i in range(PEER_TOPK):
        out = jnp.where(pos == i, table[i:i + 1, :], out)
    return out


def _peer_topk_body(sc_ref, idx_ref, g_ref):
    for h in range(PEER_HEADS):
        s0, i0 = _top16(sc_ref[2 * h], PEER_N_KEYS)
        s1, i1 = _top16(sc_ref[2 * h + 1], PEER_N_KEYS)
        cand = jnp.concatenate([s0[i:i + 1, :] + s1 for i in range(PEER_TOPK)], axis=0)
        best, pos = _top16(cand, PEER_TOPK * PEER_TOPK)
        expert = (_select_rows(i0, jnp.right_shift(pos, 4)) * PEER_N_KEYS
                  + _select_rows(i1, jnp.bitwise_and(pos, PEER_TOPK - 1)))
        e = jnp.exp(best - best[0:1, :])
        hs = slice(h * PEER_TOPK, (h + 1) * PEER_TOPK)
        idx_ref[hs, :] = expert * ROW_SUBLANES
        g_ref[hs, :] = e / jnp.sum(e, axis=0, keepdims=True)


def _peer_topk(scores, *, tb, interpret):
    nhp, nk, n = scores.shape
    return pl.pallas_call(
        _peer_topk_body,
        grid=(n // tb,),
        in_specs=[pl.BlockSpec((nhp, nk, tb), lambda i: (0, 0, i))],
        out_specs=[pl.BlockSpec((PEER_SLOTS, tb), lambda i: (0, i))] * 2,
        out_shape=[jax.ShapeDtypeStruct((PEER_SLOTS, n), jnp.int32), jax.ShapeDtypeStruct((PEER_SLOTS, n), F32)],
        compiler_params=_cparams(("parallel",)),
        name="peer_topk",
        interpret=interpret,
    )(scores)


def _pack_table(t):
    e = t.shape[0]
    bits = lax.bitcast_convert_type(t.astype(BF16), jnp.uint16).astype(jnp.uint32)
    bits = bits.reshape(e, 2, ROW_SUBLANES, LANES)
    return (bits[:, 0] | (bits[:, 1] << 16)).reshape(e * ROW_SUBLANES, LANES)


def _gather_pair(idx_ref, tab_ref, g_scr, t, slot):
    for d in range(2):
        for k in range(PEER_SLOTS):
            off = pl.multiple_of(idx_ref[t + d, k], ROW_SUBLANES)
            g_scr[slot, k * ROW_SUBLANES:(k + 1) * ROW_SUBLANES, d * LANES:(d + 1) * LANES] = (
                tab_ref[pl.ds(off, ROW_SUBLANES), :])


def _token_loop(idx_ref, tab_ref, g_scr, compute_pair, tb, pairs_per_step):
    _gather_pair(idx_ref, tab_ref, g_scr, 0, 0)

    def body(i, carry):
        t0 = 2 * pairs_per_step * i
        for p in range(pairs_per_step):
            t = t0 + 2 * p
            _gather_pair(idx_ref, tab_ref, g_scr, jnp.minimum(t + 2, tb - 2), (p + 1) % 2)
            compute_pair(t, p % 2)
        return carry

    lax.fori_loop(0, tb // (2 * pairs_per_step), body, 0)


def _piece_mask():
    r = lax.broadcasted_iota(jnp.int32, (ROW_PIECES, PEER_SLOTS * ROW_PIECES), 0)
    c = lax.broadcasted_iota(jnp.int32, (ROW_PIECES, PEER_SLOTS * ROW_PIECES), 1)
    return (c % ROW_PIECES == r).astype(F32)


def _peer_u_body(idx_ref, x8_ref, g_ref, tab_ref, sel_ref, selt_ref, w_ref, g_scr, r_scr, *, tb):
    dmask = _piece_mask()
    first = lax.broadcasted_iota(jnp.int32, (2 * ROW_PIECES, LANES), 0) < ROW_PIECES

    def compute_pair(t, slot):
        rows = pltpu.bitcast(g_scr[slot], BF16)
        x16 = x8_ref[pl.ds(pl.multiple_of(t * ROW_PIECES, 2 * ROW_PIECES), 2 * ROW_PIECES), :]
        lhs = jnp.concatenate([jnp.where(first, x16, 0.0), jnp.where(first, 0.0, x16)], axis=1).astype(BF16)
        r = _dot_nt(lhs, rows)
        r_scr[pl.ds(t, 1), :] = jnp.sum(r[:ROW_PIECES] * dmask, axis=0, keepdims=True)
        r_scr[pl.ds(t + 1, 1), :] = jnp.sum(r[ROW_PIECES:] * dmask, axis=0, keepdims=True)

    _token_loop(idx_ref, tab_ref, g_scr, compute_pair, tb, pairs_per_step=16)
    hi, lo = _split2(r_scr[...])
    s = _dot(hi, sel_ref[...]) + _dot(lo, sel_ref[...])
    act = 0.5 * s * (1.0 + lax.erf(s * (2.0 ** -0.5)))
    w = (g_ref[...] * act).astype(BF16)
    w_ref[...] = _dot(w, selt_ref[...])


def _peer_v_body(idx_ref, w_ref, x1_ref, tab_ref, o_ref, g_scr, o8_scr, *, tb):
    dmask = _piece_mask()

    def compute_pair(t, slot):
        rows = pltpu.bitcast(g_scr[slot], BF16)
        lhs = jnp.concatenate([w_ref[pl.ds(t, 1), :] * dmask, w_ref[pl.ds(t + 1, 1), :] * dmask], axis=0).astype(BF16)
        o = _dot(lhs, rows)
        base = pl.multiple_of(t * ROW_PIECES, 2 * ROW_PIECES)
        o8_scr[pl.ds(base, ROW_PIECES), :] = o[:ROW_PIECES, :LANES]
        o8_scr[pl.ds(base + ROW_PIECES, ROW_PIECES), :] = o[ROW_PIECES:, LANES:]

    _token_loop(idx_ref, tab_ref, g_scr, compute_pair, tb, pairs_per_step=8)
    for j in range(ROW_PIECES):
        cs = slice(_piece_offset(j), _piece_offset(j) + LANES)
        o_ref[:, cs] = x1_ref[:, cs] + o8_scr[pl.ds(j, tb, stride=ROW_PIECES), :]


def _peer_u(idx, x8, gates, table, *, tb, interpret):
    n = idx.shape[0]
    sel_np = (np.arange(PEER_SLOTS * ROW_PIECES)[:, None] // ROW_PIECES == np.arange(PEER_SLOTS)[None, :])
    sel = jnp.asarray(sel_np.astype(np.float32), dtype=BF16)
    return pl.pallas_call(
        functools.partial(_peer_u_body, tb=tb),
        grid=(n // tb,),
        in_specs=[pl.BlockSpec((tb, PEER_SLOTS), lambda i: (i, 0), memory_space=pltpu.SMEM),
                  pl.BlockSpec((tb * ROW_PIECES, LANES), lambda i: (i, 0)),
                  pl.BlockSpec((tb, PEER_SLOTS), lambda i: (i, 0)),
                  _resident(table.shape), _resident(sel.shape), _resident(sel.T.shape)],
        out_specs=pl.BlockSpec((tb, PEER_SLOTS * ROW_PIECES), lambda i: (i, 0)),
        out_shape=jax.ShapeDtypeStruct((n, PEER_SLOTS * ROW_PIECES), F32),
        scratch_shapes=[pltpu.VMEM((2, PEER_SLOTS * ROW_SUBLANES, 2 * LANES), jnp.uint32),
                        pltpu.VMEM((tb, PEER_SLOTS * ROW_PIECES), F32)],
        compiler_params=_cparams(("parallel",)),
        name="peer_u",
        interpret=interpret,
    )(idx, x8, gates, table, sel, sel.T)


def _peer_v(idx, w_exp, x1, table, *, tb, interpret):
    n = idx.shape[0]
    return pl.pallas_call(
        functools.partial(_peer_v_body, tb=tb),
        grid=(n // tb,),
        in_specs=[pl.BlockSpec((tb, PEER_SLOTS), lambda i: (i, 0), memory_space=pltpu.SMEM),
                  pl.BlockSpec((tb, PEER_SLOTS * ROW_PIECES), lambda i: (i, 0)),
                  pl.BlockSpec((tb, D_MODEL), lambda i: (i, 0)),
                  _resident(table.shape)],
        out_specs=pl.BlockSpec((tb, D_MODEL), lambda i: (i, 0)),
        out_shape=jax.ShapeDtypeStruct((n, D_MODEL), F32),
        scratch_shapes=[pltpu.VMEM((2, PEER_SLOTS * ROW_SUBLANES, 2 * LANES), jnp.uint32),
                        pltpu.VMEM((tb * ROW_PIECES, LANES), F32)],
        compiler_params=_cparams(("parallel",)),
        name="peer_v",
        interpret=interpret,
    )(idx, w_exp, x1, table)


def _layer(x2, mem2, p, *, layer, batch, seq, mem_len, tm, attn_blk_q, attn_blk_k, peer_tb, interpret):
    lam_init = 0.8 - 0.6 * math.exp(-0.3 * layer)
    (daq, dak, dav, glq, glk, glv, glvt, glr, gla, xaq, gates) = _in_proj(
        x2, p["mix_norm_w"], p["w_in"], p["da_q_norm_w"], p["da_k_norm_w"], p["xa_q_norm_w"], p["gla_w_gate"],
        p["gla_b_gate"], seq=seq, tm=tm, interpret=interpret)
    oda = _diff_attn(daq, dak, dav, p["da_lambda"], p["da_out_norm_w"], batch=batch, seq=seq, blk_q=attn_blk_q,
                     blk_k=attn_blk_k, lam_init=lam_init, interpret=interpret)
    ogla = _gla(glq, glk, gla, glv, glvt, glr, p["gla_out_norm_w"], batch=batch, seq=seq, interpret=interpret)
    mk, mv = _mem_kv(mem2, p["mem_norm_w"], p["w_mem_kv"], p["xa_k_norm_w"], batch=batch, mem_len=mem_len,
                     interpret=interpret)
    x1, xnp, scores = _merge(x2, oda, ogla, xaq, gates, mk, mv, p["w_br_da"], p["w_br_gla"], p["w_br_xa"], p["w_out"],
                             p["ffn_norm_w"], p["peer_w_q"], p["peer_sub_keys"], seq=seq, mem_len=mem_len, tm=tm,
                             interpret=interpret)
    idx_t, g_t = _peer_topk(scores, tb=peer_tb, interpret=interpret)
    idx = idx_t.T
    n = x2.shape[0]
    w_exp = _peer_u(idx, xnp.reshape(n * ROW_PIECES, LANES), g_t.T, _pack_table(p["peer_u"]), tb=peer_tb,
                    interpret=interpret)
    return _peer_v(idx, w_exp, x1, _pack_table(p["peer_v"]), tb=peer_tb, interpret=interpret)


_PARAM_NAMES = ("mix_norm_w", "w_in", "da_q_norm_w", "da_k_norm_w", "da_lambda", "da_out_norm_w", "gla_w_gate",
                "gla_b_gate", "gla_out_norm_w", "mem_norm_w", "w_mem_kv", "xa_q_norm_w", "xa_k_norm_w", "w_br_da",
                "w_br_gla", "w_br_xa", "w_out", "ffn_norm_w", "peer_w_q", "peer_sub_keys", "peer_u", "peer_v")


def _forward(x, mem, params, *, tm=256, attn_blk_q=512, attn_blk_k=1024, peer_tb=128, interpret=False):
    batch, seq, d = x.shape
    mem_len = mem.shape[1]
    x2 = x.reshape(batch * seq, d)
    mem2 = mem.reshape(batch * mem_len, d)
    depth = params["w_in"].shape[0]
    for layer in range(depth):
        p = {name: params[name][layer] for name in _PARAM_NAMES}
        x2 = _layer(x2, mem2, p, layer=layer, batch=batch, seq=seq, mem_len=mem_len, tm=tm, attn_blk_q=attn_blk_q,
                    attn_blk_k=attn_blk_k, peer_tb=peer_tb, interpret=interpret)
    return x2.reshape(batch, seq, d)


def kernel(x, mem, mix_norm_w, w_in, da_q_norm_w, da_k_norm_w, da_lambda, da_out_norm_w, gla_w_gate, gla_b_gate,
           gla_out_norm_w, mem_norm_w, w_mem_kv, xa_q_norm_w, xa_k_norm_w, w_br_da, w_br_gla, w_br_xa, w_out,
           ffn_norm_w, peer_w_q, peer_sub_keys, peer_u, peer_v):
    params = dict(zip(_PARAM_NAMES, (mix_norm_w, w_in, da_q_norm_w, da_k_norm_w, da_lambda, da_out_norm_w, gla_w_gate,
                                     gla_b_gate, gla_out_norm_w, mem_norm_w, w_mem_kv, xa_q_norm_w, xa_k_norm_w,
                                     w_br_da, w_br_gla, w_br_xa, w_out, ffn_norm_w, peer_w_q, peer_sub_keys, peer_u,
                                     peer_v)))
    return _forward(x, mem, params)
```

```python
import functools
import math

import jax
import jax.numpy as jnp
import numpy as np
from jax import lax
from jax.experimental import pallas as pl
from jax.experimental.pallas import tpu as pltpu

F32 = jnp.float32
BF16 = jnp.bfloat16

LANES = 128
SUBLANES = 8
VMEM_LIMIT_BYTES = 56 * 1024 * 1024

D_MODEL = 1024
DA_HEADS = 4
DA_HEAD_DIM = 64
DA_V_DIM = 2 * DA_HEAD_DIM
GLA_HEADS = 4
GLA_DK = 64
GLA_DV = 128
GLA_GATE_RANK = 16
GLA_TAU = 16.0
XA_HEADS = 4
XA_HEAD_DIM = 128
N_BRANCH = 3
PEER_HEADS = 8
PEER_N_KEYS = 128
PEER_QDIM = 256
PEER_TOPK = 16
RMS_EPS = 1e-6
LOG2E = math.log2(math.e)

DA_W = DA_HEADS * 2 * DA_HEAD_DIM
GLA_QW = GLA_HEADS * GLA_DK
GLA_VW = GLA_HEADS * GLA_DV
XA_W = XA_HEADS * XA_HEAD_DIM
GATE_W = N_BRANCH * D_MODEL
PEER_SLOTS = PEER_HEADS * PEER_TOPK

C_DAQ = 0
C_DAK = C_DAQ + DA_W
C_DAV = C_DAK + DA_W
C_GLQ = C_DAV + DA_W
C_GLK = C_GLQ + GLA_QW
C_GLV = C_GLK + GLA_QW
C_GLR = C_GLV + GLA_VW
C_XAQ = C_GLR + GLA_VW
C_GATE = C_XAQ + XA_W
C_GLG = C_GATE + GATE_W
C_END = C_GLG + LANES

GLA_CHUNK = 128
GLA_LEVELS = 7

ROW_SUBLANES = 4
ROW_PIECES = 2 * ROW_SUBLANES


def _cparams(sem):
    return pltpu.CompilerParams(dimension_semantics=sem, vmem_limit_bytes=VMEM_LIMIT_BYTES)


def _resident(shape):
    nd = len(shape)
    return pl.BlockSpec(shape, lambda *_: (0,) * nd, pipeline_mode=pl.Buffered(1))


def _dot(a, b):
    return jnp.dot(a, b, preferred_element_type=F32)


def _dot_nt(a, b):
    return lax.dot_general(a, b, (((1,), (1,)), ((), ())), preferred_element_type=F32)


def _split2(t):
    hi = t.astype(BF16)
    lo = (t - hi.astype(F32)).astype(BF16)
    return hi, lo


def _group_sumsq(t, bd_ref):
    hi, lo = _split2(t * t)
    bd = bd_ref[...]
    return _dot(hi, bd) + _dot(lo, bd)


def _in_proj_body(x_ref, nw_ref, w_ref, wvt_ref, qw_ref, kw_ref, xw_ref, wg_ref, bg_ref, bd64_ref, bd128_ref,
                  daq_ref, dak_ref, dav_ref, glq_ref, glk_ref, glv_ref, glvt_ref, glr_ref, gla_ref, xaq_ref,
                  gate_ref, *, seq):
    x = x_ref[...]
    ms = jnp.mean(x * x, axis=-1, keepdims=True)
    xn = (x * lax.rsqrt(ms + RMS_EPS) * nw_ref[...]).astype(BF16)

    def proj(c0, width):
        return _dot(xn, w_ref[:, c0:c0 + width])

    q = proj(C_DAQ, DA_W)
    q = q * lax.rsqrt(_group_sumsq(q, bd64_ref) * (1.0 / DA_HEAD_DIM) + RMS_EPS) * qw_ref[...]
    k = proj(C_DAK, DA_W)
    k = k * lax.rsqrt(_group_sumsq(k, bd64_ref) * (1.0 / DA_HEAD_DIM) + RMS_EPS) * kw_ref[...]
    tm = x.shape[0]
    lane = lax.broadcasted_iota(jnp.int32, (1, LANES), 1)
    pos = (pl.program_id(0) * tm + lax.broadcasted_iota(jnp.int32, (tm, 1), 0)) % seq
    pos = pos.astype(F32)
    for h in range(DA_HEADS):
        hs = slice(h * DA_V_DIM, (h + 1) * DA_V_DIM)
        b0 = pos * (LOG2E * 2.0 ** (-8.0 * (h + 1) / DA_HEADS))
        p0 = b0.astype(BF16).astype(F32)
        b1 = b0 - p0
        p1 = b1.astype(BF16).astype(F32)
        p2 = b1 - p1
        for c in range(2):
            own = (lane < DA_HEAD_DIM) if c == 0 else (lane >= DA_HEAD_DIM)
            first = DA_HEAD_DIM * (1 - c)
            cs = slice((2 * h + c) * LANES, (2 * h + c + 1) * LANES)
            ones = ((lane >= first) & (lane < first + 3)).astype(F32)
            daq_ref[:, cs] = jnp.where(own, q[:, hs], ones).astype(BF16)
            feat = jnp.where(lane == first, p0, jnp.where(lane == first + 1, p1, jnp.where(lane == first + 2, p2, 0.0)))
            dak_ref[:, cs] = jnp.where(own, k[:, hs], feat).astype(BF16)
    dav_ref[...] = proj(C_DAV, DA_W).astype(BF16)

    glq_ref[...] = proj(C_GLQ, GLA_QW) * (GLA_DK ** -0.5)
    glk_ref[...] = proj(C_GLK, GLA_QW)
    glv_ref[...] = proj(C_GLV, GLA_VW).astype(BF16)
    glvt_ref[...] = _dot_nt(wvt_ref[...], xn).astype(BF16)
    r = proj(C_GLR, GLA_VW)
    glr_ref[...] = (r * jax.nn.sigmoid(r)).astype(BF16)

    g = proj(C_GLG, LANES)
    z = jnp.dot(g, wg_ref[...], preferred_element_type=F32, precision=lax.Precision.HIGHEST) + bg_ref[...]
    log_sig = jnp.minimum(z, 0.0) - jnp.log1p(jnp.exp(-jnp.abs(z)))
    gla_ref[...] = log_sig * (1.0 / GLA_TAU)

    xq = proj(C_XAQ, XA_W)
    xaq_ref[...] = (xq * lax.rsqrt(_group_sumsq(xq, bd128_ref) * (1.0 / XA_HEAD_DIM) + RMS_EPS) * xw_ref[...]).astype(BF16)

    for c in range(0, GATE_W, 512):
        gate_ref[:, c:c + 512] = jax.nn.sigmoid(proj(C_GATE + c, 512)).astype(BF16)


def _block_diag_ones(width, group):
    idx = np.arange(width) // group
    return jnp.asarray((idx[:, None] == idx[None, :]).astype(np.float32), dtype=BF16)


def _in_proj(x2, mix_norm_w, w_in, da_q_norm_w, da_k_norm_w, xa_q_norm_w, gla_w_gate, gla_b_gate, *, seq, tm,
             interpret):
    n = x2.shape[0]
    c_glg_src = C_GLR + GLA_VW
    w = jnp.concatenate(
        [w_in[:, :c_glg_src], w_in[:, c_glg_src + GLA_GATE_RANK:], w_in[:, c_glg_src:c_glg_src + GLA_GATE_RANK],
         jnp.zeros((D_MODEL, LANES - GLA_GATE_RANK), w_in.dtype)], axis=1).astype(BF16)
    wvt = w_in[:, C_GLV:C_GLV + GLA_VW].T.astype(BF16)
    qw = jnp.tile(da_q_norm_w, 2 * DA_HEADS)[None, :] * (DA_HEAD_DIM ** -0.5 * LOG2E)
    kw = jnp.tile(da_k_norm_w, 2 * DA_HEADS)[None, :]
    xw = jnp.tile(xa_q_norm_w, XA_HEADS)[None, :] * (XA_HEAD_DIM ** -0.5)
    wg = jnp.zeros((LANES, GLA_QW), F32).at[:GLA_GATE_RANK].set(gla_w_gate)
    bg = gla_b_gate[None, :]
    row = lambda width: pl.BlockSpec((tm, width), lambda i: (i, 0))
    out_widths = [(2 * DA_W, BF16), (2 * DA_W, BF16), (DA_W, BF16), (GLA_QW, F32), (GLA_QW, F32), (GLA_VW, BF16)]
    out_shape = [jax.ShapeDtypeStruct((n, wd), dt) for wd, dt in out_widths]
    out_specs = [row(wd) for wd, _ in out_widths]
    out_shape.append(jax.ShapeDtypeStruct((GLA_VW, n), BF16))
    out_specs.append(pl.BlockSpec((GLA_VW, tm), lambda i: (0, i)))
    for wd, dt in [(GLA_VW, BF16), (GLA_QW, F32), (XA_W, BF16), (GATE_W, BF16)]:
        out_shape.append(jax.ShapeDtypeStruct((n, wd), dt))
        out_specs.append(row(wd))
    return pl.pallas_call(
        functools.partial(_in_proj_body, seq=seq),
        grid=(n // tm,),
        in_specs=[row(D_MODEL), _resident((1, D_MODEL)), _resident((D_MODEL, C_END)), _resident((GLA_VW, D_MODEL)),
                  _resident((1, DA_W)), _resident((1, DA_W)), _resident((1, XA_W)), _resident((LANES, GLA_QW)),
                  _resident((1, GLA_QW)), _resident((DA_W, DA_W)), _resident((XA_W, XA_W))],
        out_specs=out_specs,
        out_shape=out_shape,
        compiler_params=_cparams(("parallel",)),
        name="in_proj",
        interpret=interpret,
    )(x2, mix_norm_w[None, :], w, wvt, qw, kw, xw, wg, bg, _block_diag_ones(DA_W, DA_HEAD_DIM),
      _block_diag_ones(XA_W, XA_HEAD_DIM))


def _diff_attn_body(qi_tab, kj_tab, q_ref, k_ref, v_ref, lam_ref, ow_ref, o_ref, m_scr, acc_scr, *, blk_q, blk_k,
                    lam_init):
    p = pl.program_id(2)
    qi = qi_tab[p]
    kj = kj_tab[p]
    last = (qi * blk_q) // blk_k

    @pl.when(kj == 0)
    def _():
        m_scr[...] = jnp.full(m_scr.shape, -jnp.inf, F32)
        acc_scr[...] = jnp.zeros(acc_scr.shape, F32)

    v = v_ref[...]
    v_ones = jnp.concatenate([v, jnp.ones_like(v)], axis=1)

    def component(c, causal):
        cs = slice(c * LANES, (c + 1) * LANES)
        s = _dot_nt(q_ref[:, cs], k_ref[:, cs])
        if causal:
            row = lax.broadcasted_iota(jnp.int32, (blk_q, 1), 0) + qi * blk_q
            col = lax.broadcasted_iota(jnp.int32, (1, blk_k), 1) + kj * blk_k
            s = jnp.where(col <= row, s, -jnp.inf)
        m_prev = m_scr[c]
        m_new = jnp.maximum(m_prev, jnp.max(s, axis=-1, keepdims=True))
        acc_scr[c] = jnp.exp2(m_prev - m_new) * acc_scr[c] + _dot(jnp.exp2(s - m_new).astype(BF16), v_ones)
        m_scr[c] = m_new

    @pl.when(kj < last)
    def _():
        component(0, False)
        component(1, False)

    @pl.when(kj == last)
    def _():
        component(0, True)
        component(1, True)
        lv = lam_ref[...]
        lam = (jnp.exp(jnp.sum(lv[0:1] * lv[1:2], axis=-1, keepdims=True))
               - jnp.exp(jnp.sum(lv[2:3] * lv[3:4], axis=-1, keepdims=True)) + lam_init)
        a0 = acc_scr[0]
        a1 = acc_scr[1]
        o = a0[:, :DA_V_DIM] / a0[:, DA_V_DIM:] - lam * (a1[:, :DA_V_DIM] / a1[:, DA_V_DIM:])
        ms = jnp.mean(o * o, axis=-1, keepdims=True)
        o_ref[...] = (o * lax.rsqrt(ms + RMS_EPS) * ow_ref[...] * (1.0 - lam_init)).astype(BF16)


def _diff_attn(daq, dak, dav, da_lambda, da_out_norm_w, *, batch, seq, blk_q, blk_k, lam_init, interpret):
    assert blk_k % blk_q == 0 and seq % blk_k == 0
    nq = seq // blk_q
    nk = seq // blk_k
    pairs = [(qi, kj) for qi in range(nq) for kj in range((qi * blk_q) // blk_k + 1)]
    qi_tab = jnp.asarray([p[0] for p in pairs], jnp.int32)
    kj_tab = jnp.asarray([p[1] for p in pairs], jnp.int32)
    qmap = lambda b, h, p, qt, kt: (b * nq + qt[p], h)
    kmap = lambda b, h, p, qt, kt: (b * nk + kt[p], h)
    grid_spec = pltpu.PrefetchScalarGridSpec(
        num_scalar_prefetch=2,
        grid=(batch, DA_HEADS, len(pairs)),
        in_specs=[pl.BlockSpec((blk_q, 2 * LANES), qmap), pl.BlockSpec((blk_k, 2 * LANES), kmap),
                  pl.BlockSpec((blk_k, DA_V_DIM), kmap),
                  pl.BlockSpec((4, DA_HEAD_DIM), lambda b, h, p, qt, kt: (0, 0)),
                  pl.BlockSpec((1, DA_V_DIM), lambda b, h, p, qt, kt: (0, 0))],
        out_specs=pl.BlockSpec((blk_q, DA_V_DIM), qmap),
        scratch_shapes=[pltpu.VMEM((2, blk_q, 1), F32), pltpu.VMEM((2, blk_q, 2 * DA_V_DIM), F32)],
    )
    return pl.pallas_call(
        functools.partial(_diff_attn_body, blk_q=blk_q, blk_k=blk_k, lam_init=lam_init),
        grid_spec=grid_spec,
        out_shape=jax.ShapeDtypeStruct((batch * seq, DA_HEADS * DA_V_DIM), BF16),
        compiler_params=_cparams(("parallel", "parallel", "arbitrary")),
        name="diff_attn",
        interpret=interpret,
    )(qi_tab, kj_tab, daq, dak, dav, da_lambda, da_out_norm_w[None, :])


def _gla_constants():
    c = GLA_CHUNK
    t = np.arange(c)[:, None]
    u = np.arange(c)[None, :]
    lmats, masks = [], []
    for lev in range(GLA_LEVELS):
        m = 1 << lev
        second = (t % (2 * m)) >= m
        first = ~second
        bnd = (t // m) * m
        lmats.append(second & (u >= bnd) & (u <= t))
    for lev in range(GLA_LEVELS):
        m = 1 << lev
        first = (t % (2 * m)) < m
        end = (t // m) * m + m - 1
        lmats.append(first & (u > t) & (u <= end))
    lmats.append(u <= t)
    lmats.append(u > t)
    for lev in range(GLA_LEVELS):
        m = 1 << lev
        s = np.arange(c)[None, :]
        masks.append(((t // (2 * m)) == (s // (2 * m))) & ((t % (2 * m)) >= m) & ((s % (2 * m)) < m))
    masks.append(t == np.arange(c)[None, :])
    lall = jnp.asarray(np.concatenate(lmats, axis=0).astype(np.float32), dtype=BF16)
    mall = jnp.asarray(np.stack(masks, axis=0).astype(np.float32))
    return lall, mall


def _gla_body(q_ref, k_ref, la_ref, v_ref, vt_ref, r_ref, lall_ref, mall_ref, ow_ref, o_ref, state_scr):
    c = GLA_CHUNK

    @pl.when(pl.program_id(1) == 0)
    def _():
        state_scr[...] = jnp.zeros(state_scr.shape, F32)

    q = q_ref[...]
    k = k_ref[...]
    g = la_ref[...]
    g1 = g.astype(BF16)
    r1 = g - g1.astype(F32)
    g2 = r1.astype(BF16)
    g3 = (r1 - g2.astype(F32)).astype(BF16)
    lall = lall_ref[...]
    e_all = jnp.exp(_dot(lall, g1) + _dot(lall, g2) + _dot(lall, g3))

    def rows(i):
        return e_all[i * c:(i + 1) * c]

    lane = lax.broadcasted_iota(jnp.int32, (1, GLA_QW), 1)
    head_masks = [(lane // GLA_DK == h).astype(F32) for h in range(GLA_HEADS)]
    a = [jnp.zeros((c, c), F32) for _ in range(GLA_HEADS)]
    for lev in range(GLA_LEVELS + 1):
        if lev < GLA_LEVELS:
            ql = q * rows(lev)
            kl = (k * rows(GLA_LEVELS + lev)).astype(BF16)
        else:
            ql = q
            kl = k.astype(BF16)
        mask = mall_ref[lev]
        for h in range(GLA_HEADS):
            a[h] = a[h] + mask * _dot_nt((ql * head_masks[h]).astype(BF16), kl)

    e_b = rows(2 * GLA_LEVELS)
    q_dec = q * e_b
    k_dec = (k * rows(2 * GLA_LEVELS + 1)).astype(BF16)
    chunk_decay = e_b[c - 1:c, :]
    for h in range(GLA_HEADS):
        vs = slice(h * GLA_DV, (h + 1) * GLA_DV)
        st = state_scr[h]
        o = _dot(a[h].astype(BF16), v_ref[:, vs]) + _dot_nt((q_dec * head_masks[h]).astype(BF16), st.astype(BF16))
        state_scr[h] = st * chunk_decay + _dot(vt_ref[vs, :], k_dec)
        ms = jnp.mean(o * o, axis=-1, keepdims=True)
        o_ref[:, vs] = (o * lax.rsqrt(ms + RMS_EPS) * ow_ref[...] * r_ref[:, vs].astype(F32)).astype(BF16)


def _gla(glq, glk, gla, glv, glvt, glr, gla_out_norm_w, *, batch, seq, interpret):
    c = GLA_CHUNK
    nc = seq // c
    lall, mall = _gla_constants()
    row = lambda width: pl.BlockSpec((c, width), lambda b, i: (b * nc + i, 0))
    return pl.pallas_call(
        _gla_body,
        grid=(batch, nc),
        in_specs=[row(GLA_QW), row(GLA_QW), row(GLA_QW), row(GLA_VW),
                  pl.BlockSpec((GLA_VW, c), lambda b, i: (0, b * nc + i)), row(GLA_VW),
                  _resident(lall.shape), _resident(mall.shape), _resident((1, GLA_DV))],
        out_specs=row(GLA_VW),
        out_shape=jax.ShapeDtypeStruct((batch * seq, GLA_VW), BF16),
        scratch_shapes=[pltpu.VMEM((GLA_HEADS, GLA_DV, GLA_QW), F32)],
        compiler_params=_cparams(("parallel", "arbitrary")),
        name="gla",
        interpret=interpret,
    )(glq, glk, gla, glv, glvt, glr, lall, mall, gla_out_norm_w[None, :])


def _mem_kv_body(mem_ref, nw_ref, w_ref, kw_ref, bd_ref, mk_ref, mv_ref):
    x = mem_ref[...]
    ms = jnp.mean(x * x, axis=-1, keepdims=True)
    xn = (x * lax.rsqrt(ms + RMS_EPS) * nw_ref[...]).astype(BF16)
    k = _dot(xn, w_ref[:, :XA_W])
    mk_ref[...] = (k * lax.rsqrt(_group_sumsq(k, bd_ref) * (1.0 / XA_HEAD_DIM) + RMS_EPS) * kw_ref[...]).astype(BF16)
    mv_ref[...] = _dot(xn, w_ref[:, XA_W:]).astype(BF16)


def _mem_kv(mem2, mem_norm_w, w_mem_kv, xa_k_norm_w, *, batch, mem_len, interpret):
    blk = pl.BlockSpec((mem_len, XA_W), lambda b: (b, 0))
    return pl.pallas_call(
        _mem_kv_body,
        grid=(batch,),
        in_specs=[pl.BlockSpec((mem_len, D_MODEL), lambda b: (b, 0)), _resident((1, D_MODEL)),
                  _resident((D_MODEL, 2 * XA_W)), _resident((1, XA_W)), _resident((XA_W, XA_W))],
        out_specs=[blk, blk],
        out_shape=[jax.ShapeDtypeStruct((batch * mem_len, XA_W), BF16)] * 2,
        compiler_params=_cparams(("parallel",)),
        name="mem_kv",
        interpret=interpret,
    )(mem2, mem_norm_w[None, :], w_mem_kv.astype(BF16), jnp.tile(xa_k_norm_w, XA_HEADS)[None, :],
      _block_diag_ones(XA_W, XA_HEAD_DIM))


def _piece_offset(j):
    return (j % 2) * (D_MODEL // 2) + (j // 2) * LANES


def _merge_body(x_ref, oda_ref, ogla_ref, xaq_ref, gate_ref, mk_ref, mv_ref, wda_ref, wgla_ref, wxa_ref, wout_ref,
                fw_ref, wq_ref, keys_ref, x1_ref, x8_ref, sc_ref):
    tm = x_ref.shape[0]
    br_xa = None
    for h in range(XA_HEADS):
        hs = slice(h * XA_HEAD_DIM, (h + 1) * XA_HEAD_DIM)
        s = _dot_nt(xaq_ref[:, hs], mk_ref[:, hs])
        s = s - jnp.max(s, axis=-1, keepdims=True)
        p = jnp.exp(s)
        p = p / jnp.sum(p, axis=-1, keepdims=True)
        o = _dot(p.astype(BF16), mv_ref[:, hs])
        t = _dot(o.astype(BF16), wxa_ref[hs, :])
        br_xa = t if br_xa is None else br_xa + t
    merged = (gate_ref[:, 0:D_MODEL].astype(F32) * _dot(oda_ref[...], wda_ref[...])
              + gate_ref[:, D_MODEL:2 * D_MODEL].astype(F32) * _dot(ogla_ref[...], wgla_ref[...])
              + gate_ref[:, 2 * D_MODEL:].astype(F32) * br_xa)
    x1 = x_ref[...] + _dot(merged.astype(BF16), wout_ref[...])
    x1_ref[...] = x1
    ms = jnp.mean(x1 * x1, axis=-1, keepdims=True)
    xn = x1 * lax.rsqrt(ms + RMS_EPS) * fw_ref[...]
    xnb = xn.astype(BF16)
    for j in range(ROW_PIECES):
        off = _piece_offset(j)
        x8_ref[pl.ds(j, tm, stride=ROW_PIECES), :] = xn[:, off:off + LANES]
    pq = _dot(xnb, wq_ref[...])
    half = PEER_QDIM // 2
    for hp in range(2 * PEER_HEADS):
        q_hi, q_lo = _split2(pq[:, hp * half:(hp + 1) * half])
        keys = keys_ref[hp % 2]
        sc_ref[hp] = _dot_nt(keys, q_hi) + _dot_nt(keys, q_lo)


def _merge(x2, oda, ogla, xaq, gates, mk, mv, w_br_da, w_br_gla, w_br_xa, w_out, ffn_norm_w, peer_w_q, peer_sub_keys,
           *, seq, mem_len, tm, interpret):
    n = x2.shape[0]
    spb = seq // tm
    row = lambda width: pl.BlockSpec((tm, width), lambda i: (i, 0))
    mem_blk = pl.BlockSpec((mem_len, XA_W), lambda i: (i // spb, 0))
    nhp = 2 * PEER_HEADS
    return pl.pallas_call(
        _merge_body,
        grid=(n // tm,),
        in_specs=[row(D_MODEL), row(DA_W), row(GLA_VW), row(XA_W), row(GATE_W), mem_blk, mem_blk,
                  _resident((DA_W, D_MODEL)), _resident((GLA_VW, D_MODEL)), _resident((XA_W, D_MODEL)),
                  _resident((D_MODEL, D_MODEL)), _resident((1, D_MODEL)),
                  _resident((D_MODEL, PEER_HEADS * PEER_QDIM)), _resident((2, PEER_N_KEYS, PEER_QDIM // 2))],
        out_specs=[row(D_MODEL), pl.BlockSpec((tm * ROW_PIECES, LANES), lambda i: (i, 0)),
                   pl.BlockSpec((nhp, PEER_N_KEYS, tm), lambda i: (0, 0, i))],
        out_shape=[jax.ShapeDtypeStruct((n, D_MODEL), F32), jax.ShapeDtypeStruct((n * ROW_PIECES, LANES), F32),
                   jax.ShapeDtypeStruct((nhp, PEER_N_KEYS, n), F32)],
        compiler_params=_cparams(("parallel",)),
        name="merge",
        interpret=interpret,
    )(x2, oda, ogla, xaq, gates, mk, mv, w_br_da.astype(BF16), w_br_gla.astype(BF16), w_br_xa.astype(BF16),
      w_out.astype(BF16), ffn_norm_w[None, :], peer_w_q.astype(BF16), peer_sub_keys.astype(BF16))


def _top16(x, code):
    vals, codes = [], []
    for _ in range(PEER_TOPK):
        m = jnp.max(x, axis=0, keepdims=True)
        win = jnp.min(jnp.where(x == m, code, jnp.inf), axis=0, keepdims=True)
        vals.append(m)
        codes.append(win)
        x = jnp.where(code == win, -jnp.inf, x)
    return jnp.concatenate(vals, axis=0), jnp.concatenate(codes, axis=0).astype(jnp.int32)


def _select_rows(table, pos):
    out = jnp.zeros(pos.shape, table.dtype)
    for i in range(PEER_TOPK):
        out = jnp.where(pos == i, table[i:i + 1, :], out)
    return out


def _pair_candidates():
    blocks = [(0, 1, 0, PEER_TOPK)] + [(i, i + 1, 0, SUBLANES) for i in range(1, SUBLANES)]
    blocks.append((SUBLANES, PEER_TOPK, 0, 1))
    codes = np.concatenate([np.array([i * PEER_TOPK + j for i in range(i0, i1) for j in range(j0, j1)])
                            for i0, i1, j0, j1 in blocks])
    return blocks, codes.astype(np.float32)[:, None]


def _peer_topk_body(sc_ref, pair_code_ref, idx_ref, g_ref):
    blocks, _ = _pair_candidates()
    tb = idx_ref.shape[1]
    key_code = lax.broadcasted_iota(jnp.int32, (PEER_N_KEYS, tb), 0).astype(F32)
    pair_code = jnp.broadcast_to(pair_code_ref[...], (pair_code_ref.shape[0], tb))
    experts, gates = [], []
    for h in range(PEER_HEADS):
        s0, i0 = _top16(sc_ref[2 * h], key_code)
        s1, i1 = _top16(sc_ref[2 * h + 1], key_code)
        cand = jnp.concatenate([s0[a0:a1, :] + s1[b0:b1, :] for a0, a1, b0, b1 in blocks], axis=0)
        best, flat = _top16(cand, pair_code)
        expert = (_select_rows(i0, jnp.right_shift(flat, 4)) * PEER_N_KEYS
                  + _select_rows(i1, jnp.bitwise_and(flat, PEER_TOPK - 1)))
        e = jnp.exp(best - best[0:1, :])
        experts.append(expert * ROW_SUBLANES)
        gates.append(e / jnp.sum(e, axis=0, keepdims=True))
    idx_ref[...] = jnp.concatenate(experts, axis=0)
    g_ref[...] = jnp.concatenate(gates, axis=0).T


def _peer_topk(scores, *, tb, interpret):
    nhp, nk, n = scores.shape
    pair_code = jnp.asarray(_pair_candidates()[1])
    return pl.pallas_call(
        _peer_topk_body,
        grid=(n // tb,),
        in_specs=[pl.BlockSpec((nhp, nk, tb), lambda i: (0, 0, i)), _resident(pair_code.shape)],
        out_specs=[pl.BlockSpec((PEER_SLOTS, tb), lambda i: (0, i)), pl.BlockSpec((tb, PEER_SLOTS), lambda i: (i, 0))],
        out_shape=[jax.ShapeDtypeStruct((PEER_SLOTS, n), jnp.int32), jax.ShapeDtypeStruct((n, PEER_SLOTS), F32)],
        compiler_params=_cparams(("parallel",)),
        name="peer_topk",
        interpret=interpret,
    )(scores, pair_code)


def _pack_table(t):
    e = t.shape[0]
    bits = lax.bitcast_convert_type(t.astype(BF16), jnp.uint16).astype(jnp.uint32)
    bits = bits.reshape(e, 2, ROW_SUBLANES, LANES)
    return (bits[:, 0] | (bits[:, 1] << 16)).reshape(e * ROW_SUBLANES, LANES)


def _gather_pair(idx_scr, tab_ref, g_scr, t, slot):
    for d in range(2):
        for k in range(PEER_SLOTS):
            off = pl.multiple_of(idx_scr.at[k][t + d], ROW_SUBLANES)
            g_scr[slot, k * ROW_SUBLANES:(k + 1) * ROW_SUBLANES, d * LANES:(d + 1) * LANES] = (
                tab_ref[pl.ds(off, ROW_SUBLANES), :])


def _token_loop(idx_scr, tab_ref, g_scr, compute_pair, tb, pairs_per_step, base):
    _gather_pair(idx_scr, tab_ref, g_scr, 0, 0)

    def body(i, carry):
        t0 = 2 * pairs_per_step * i
        for p in range(pairs_per_step):
            t = t0 + 2 * p
            _gather_pair(idx_scr, tab_ref, g_scr, jnp.minimum(t + 2, tb - 2), (p + 1) % 2)
            compute_pair(base + t, p % 2)
        return carry

    lax.fori_loop(0, tb // (2 * pairs_per_step), body, 0)


def _two_block_loop(idx_hbm, idx_a, idx_b, sem, tab_ref, g_scr, compute_pair, tb, pairs_per_step):
    i = pl.program_id(0)

    def copy(blk, scr, s):
        return pltpu.make_async_copy(idx_hbm.at[:, pl.ds(pl.multiple_of(blk * tb, LANES), tb)], scr, sem.at[s])

    @pl.when(i == 0)
    def _():
        copy(0, idx_a, 0).start()

    copy(2 * i, idx_a, 0).wait()
    copy(2 * i + 1, idx_b, 1).start()
    _token_loop(idx_a, tab_ref, g_scr, compute_pair, tb, pairs_per_step, 0)
    copy(2 * i + 1, idx_b, 1).wait()

    @pl.when(i + 1 < pl.num_programs(0))
    def _():
        copy(2 * i + 2, idx_a, 0).start()

    _token_loop(idx_b, tab_ref, g_scr, compute_pair, tb, pairs_per_step, tb)


def _piece_mask():
    r = lax.broadcasted_iota(jnp.int32, (ROW_PIECES, PEER_SLOTS * ROW_PIECES), 0)
    c = lax.broadcasted_iota(jnp.int32, (ROW_PIECES, PEER_SLOTS * ROW_PIECES), 1)
    return (c % ROW_PIECES == r).astype(F32)


def _peer_u_body(idx_hbm, x8_ref, g_ref, tab_ref, sel_ref, selt_ref, w_ref, idx_a, idx_b, sem, g_scr, r_scr, *, tb):
    dmask = _piece_mask()
    first = lax.broadcasted_iota(jnp.int32, (2 * ROW_PIECES, LANES), 0) < ROW_PIECES

    def compute_pair(t, slot):
        rows = pltpu.bitcast(g_scr[slot], BF16)
        x16 = x8_ref[pl.ds(pl.multiple_of(t * ROW_PIECES, 2 * ROW_PIECES), 2 * ROW_PIECES), :]
        lhs = jnp.concatenate([jnp.where(first, x16, 0.0), jnp.where(first, 0.0, x16)], axis=1).astype(BF16)
        r = _dot_nt(lhs, rows)
        r_scr[pl.ds(t, 1), :] = jnp.sum(r[:ROW_PIECES] * dmask, axis=0, keepdims=True)
        r_scr[pl.ds(t + 1, 1), :] = jnp.sum(r[ROW_PIECES:] * dmask, axis=0, keepdims=True)

    _two_block_loop(idx_hbm, idx_a, idx_b, sem, tab_ref, g_scr, compute_pair, tb, pairs_per_step=16)
    hi, lo = _split2(r_scr[...])
    s = _dot(hi, sel_ref[...]) + _dot(lo, sel_ref[...])
    act = 0.5 * s * (1.0 + lax.erf(s * (2.0 ** -0.5)))
    w = (g_ref[...] * act).astype(BF16)
    w_ref[...] = _dot(w, selt_ref[...])


def _peer_v_body(idx_hbm, w_ref, x1_ref, tab_ref, o_ref, idx_a, idx_b, sem, g_scr, o8_scr, *, tb):
    dmask = _piece_mask()

    def compute_pair(t, slot):
        rows = pltpu.bitcast(g_scr[slot], BF16)
        lhs = jnp.concatenate([w_ref[pl.ds(t, 1), :] * dmask, w_ref[pl.ds(t + 1, 1), :] * dmask], axis=0).astype(BF16)
        o = _dot(lhs, rows)
        base = pl.multiple_of(t * ROW_PIECES, 2 * ROW_PIECES)
        o8_scr[pl.ds(base, ROW_PIECES), :] = o[:ROW_PIECES, :LANES]
        o8_scr[pl.ds(base + ROW_PIECES, ROW_PIECES), :] = o[ROW_PIECES:, LANES:]

    _two_block_loop(idx_hbm, idx_a, idx_b, sem, tab_ref, g_scr, compute_pair, tb, pairs_per_step=16)
    for j in range(ROW_PIECES):
        cs = slice(_piece_offset(j), _piece_offset(j) + LANES)
        o_ref[:, cs] = x1_ref[:, cs] + o8_scr[pl.ds(j, 2 * tb, stride=ROW_PIECES), :]


def _idx_scratch(tb):
    return [pltpu.SMEM((PEER_SLOTS, tb), jnp.int32), pltpu.SMEM((PEER_SLOTS, tb), jnp.int32),
            pltpu.SemaphoreType.DMA((2,)), pltpu.VMEM((2, PEER_SLOTS * ROW_SUBLANES, 2 * LANES), jnp.uint32)]


def _peer_u(idx_t, x8, gates, table, *, tb, interpret):
    n = idx_t.shape[1]
    ts = 2 * tb
    sel_np = (np.arange(PEER_SLOTS * ROW_PIECES)[:, None] // ROW_PIECES == np.arange(PEER_SLOTS)[None, :])
    sel = jnp.asarray(sel_np.astype(np.float32), dtype=BF16)
    return pl.pallas_call(
        functools.partial(_peer_u_body, tb=tb),
        grid=(n // ts,),
        in_specs=[pl.BlockSpec(memory_space=pl.ANY),
                  pl.BlockSpec((ts * ROW_PIECES, LANES), lambda i: (i, 0)),
                  pl.BlockSpec((ts, PEER_SLOTS), lambda i: (i, 0)),
                  _resident(table.shape), _resident(sel.shape), _resident(sel.T.shape)],
        out_specs=pl.BlockSpec((ts, PEER_SLOTS * ROW_PIECES), lambda i: (i, 0)),
        out_shape=jax.ShapeDtypeStruct((n, PEER_SLOTS * ROW_PIECES), F32),
        scratch_shapes=_idx_scratch(tb) + [pltpu.VMEM((ts, PEER_SLOTS * ROW_PIECES), F32)],
        compiler_params=_cparams(("arbitrary",)),
        name="peer_u",
        interpret=interpret,
    )(idx_t, x8, gates, table, sel, sel.T)


def _peer_v(idx_t, w_exp, x1, table, *, tb, interpret):
    n = idx_t.shape[1]
    ts = 2 * tb
    return pl.pallas_call(
        functools.partial(_peer_v_body, tb=tb),
        grid=(n // ts,),
        in_specs=[pl.BlockSpec(memory_space=pl.ANY),
                  pl.BlockSpec((ts, PEER_SLOTS * ROW_PIECES), lambda i: (i, 0)),
                  pl.BlockSpec((ts, D_MODEL), lambda i: (i, 0)),
                  _resident(table.shape)],
        out_specs=pl.BlockSpec((ts, D_MODEL), lambda i: (i, 0)),
        out_shape=jax.ShapeDtypeStruct((n, D_MODEL), F32),
        scratch_shapes=_idx_scratch(tb) + [pltpu.VMEM((ts * ROW_PIECES, LANES), F32)],
        compiler_params=_cparams(("arbitrary",)),
        name="peer_v",
        interpret=interpret,
    )(idx_t, w_exp, x1, table)


def _layer(x2, mem2, p, *, layer, batch, seq, mem_len, tm, attn_blk_q, attn_blk_k, peer_tb, interpret):
    lam_init = 0.8 - 0.6 * math.exp(-0.3 * layer)
    (daq, dak, dav, glq, glk, glv, glvt, glr, gla, xaq, gates) = _in_proj(
        x2, p["mix_norm_w"], p["w_in"], p["da_q_norm_w"], p["da_k_norm_w"], p["xa_q_norm_w"], p["gla_w_gate"],
        p["gla_b_gate"], seq=seq, tm=tm, interpret=interpret)
    oda = _diff_attn(daq, dak, dav, p["da_lambda"], p["da_out_norm_w"], batch=batch, seq=seq, blk_q=attn_blk_q,
                     blk_k=attn_blk_k, lam_init=lam_init, interpret=interpret)
    ogla = _gla(glq, glk, gla, glv, glvt, glr, p["gla_out_norm_w"], batch=batch, seq=seq, interpret=interpret)
    mk, mv = _mem_kv(mem2, p["mem_norm_w"], p["w_mem_kv"], p["xa_k_norm_w"], batch=batch, mem_len=mem_len,
                     interpret=interpret)
    x1, x8, scores = _merge(x2, oda, ogla, xaq, gates, mk, mv, p["w_br_da"], p["w_br_gla"], p["w_br_xa"], p["w_out"],
                             p["ffn_norm_w"], p["peer_w_q"], p["peer_sub_keys"], seq=seq, mem_len=mem_len, tm=tm,
                             interpret=interpret)
    idx_t, peer_gates = _peer_topk(scores, tb=peer_tb, interpret=interpret)
    w_exp = _peer_u(idx_t, x8, peer_gates, _pack_table(p["peer_u"]), tb=peer_tb, interpret=interpret)
    return _peer_v(idx_t, w_exp, x1, _pack_table(p["peer_v"]), tb=peer_tb, interpret=interpret)


_PARAM_NAMES = ("mix_norm_w", "w_in", "da_q_norm_w", "da_k_norm_w", "da_lambda", "da_out_norm_w", "gla_w_gate",
                "gla_b_gate", "gla_out_norm_w", "mem_norm_w", "w_mem_kv", "xa_q_norm_w", "xa_k_norm_w", "w_br_da",
                "w_br_gla", "w_br_xa", "w_out", "ffn_norm_w", "peer_w_q", "peer_sub_keys", "peer_u", "peer_v")


def _forward(x, mem, params, *, tm=256, attn_blk_q=512, attn_blk_k=1024, peer_tb=128, interpret=False):
    batch, seq, d = x.shape
    mem_len = mem.shape[1]
    x2 = x.reshape(batch * seq, d)
    mem2 = mem.reshape(batch * mem_len, d)
    depth = params["w_in"].shape[0]
    for layer in range(depth):
        p = {name: params[name][layer] for name in _PARAM_NAMES}
        x2 = _layer(x2, mem2, p, layer=layer, batch=batch, seq=seq, mem_len=mem_len, tm=tm, attn_blk_q=attn_blk_q,
                    attn_blk_k=attn_blk_k, peer_tb=peer_tb, interpret=interpret)
    return x2.reshape(batch, seq, d)


def kernel(x, mem, mix_norm_w, w_in, da_q_norm_w, da_k_norm_w, da_lambda, da_out_norm_w, gla_w_gate, gla_b_gate,
           gla_out_norm_w, mem_norm_w, w_mem_kv, xa_q_norm_w, xa_k_norm_w, w_br_da, w_br_gla, w_br_xa, w_out,
           ffn_norm_w, peer_w_q, peer_sub_keys, peer_u, peer_v):
    params = dict(zip(_PARAM_NAMES, (mix_norm_w, w_in, da_q_norm_w, da_k_norm_w, da_lambda, da_out_norm_w, gla_w_gate,
                                     gla_b_gate, gla_out_norm_w, mem_norm_w, w_mem_kv, xa_q_norm_w, xa_k_norm_w,
                                     w_br_da, w_br_gla, w_br_xa, w_out, ffn_norm_w, peer_w_q, peer_sub_keys, peer_u,
                                     peer_v)))
    return _forward(x, mem, params)
```

```python
import functools
import math

import jax
import jax.numpy as jnp
import numpy as np
from jax import lax
from jax.experimental import pallas as pl
from jax.experimental.pallas import tpu as pltpu

F32 = jnp.float32
BF16 = jnp.bfloat16

LANES = 128
SUBLANES = 8
VMEM_LIMIT_BYTES = 56 * 1024 * 1024

D_MODEL = 1024
DA_HEADS = 4
DA_HEAD_DIM = 64
DA_V_DIM = 2 * DA_HEAD_DIM
GLA_HEADS = 4
GLA_DK = 64
GLA_DV = 128
GLA_GATE_RANK = 16
GLA_TAU = 16.0
XA_HEADS = 4
XA_HEAD_DIM = 128
N_BRANCH = 3
PEER_HEADS = 8
PEER_N_KEYS = 128
PEER_QDIM = 256
PEER_TOPK = 16
RMS_EPS = 1e-6
LOG2E = math.log2(math.e)

DA_W = DA_HEADS * 2 * DA_HEAD_DIM
GLA_QW = GLA_HEADS * GLA_DK
GLA_VW = GLA_HEADS * GLA_DV
XA_W = XA_HEADS * XA_HEAD_DIM
GATE_W = N_BRANCH * D_MODEL
PEER_SLOTS = PEER_HEADS * PEER_TOPK

C_DAQ = 0
C_DAK = C_DAQ + DA_W
C_DAV = C_DAK + DA_W
C_GLQ = C_DAV + DA_W
C_GLK = C_GLQ + GLA_QW
C_GLV = C_GLK + GLA_QW
C_GLR = C_GLV + GLA_VW
C_XAQ = C_GLR + GLA_VW
C_GATE = C_XAQ + XA_W
C_GLG = C_GATE + GATE_W
C_END = C_GLG + LANES

GLA_CHUNK = 128
GLA_LEVELS = 7

ROW_SUBLANES = 4
ROW_PIECES = 2 * ROW_SUBLANES
V_TOKENS_PER_STEP = 16


def _cparams(sem):
    return pltpu.CompilerParams(dimension_semantics=sem, vmem_limit_bytes=VMEM_LIMIT_BYTES)


def _resident(shape):
    nd = len(shape)
    return pl.BlockSpec(shape, lambda *_: (0,) * nd, pipeline_mode=pl.Buffered(1))


def _dot(a, b):
    return jnp.dot(a, b, preferred_element_type=F32)


def _dot_nt(a, b):
    return lax.dot_general(a, b, (((1,), (1,)), ((), ())), preferred_element_type=F32)


def _split2(t):
    hi = t.astype(BF16)
    lo = (t - hi.astype(F32)).astype(BF16)
    return hi, lo


def _group_sumsq(t, bd_ref):
    hi, lo = _split2(t * t)
    bd = bd_ref[...]
    return _dot(hi, bd) + _dot(lo, bd)


def _in_proj_body(x_ref, nw_ref, w_ref, wvt_ref, qw_ref, kw_ref, xw_ref, wg_ref, bg_ref, bd64_ref, bd128_ref,
                  daq_ref, dak_ref, dav_ref, glq_ref, glk_ref, glv_ref, glvt_ref, glr_ref, gla_ref, xaq_ref,
                  gate_ref, *, seq):
    x = x_ref[...]
    ms = jnp.mean(x * x, axis=-1, keepdims=True)
    xn = (x * lax.rsqrt(ms + RMS_EPS) * nw_ref[...]).astype(BF16)

    def proj(c0, width):
        return _dot(xn, w_ref[:, c0:c0 + width])

    q = proj(C_DAQ, DA_W)
    q = q * lax.rsqrt(_group_sumsq(q, bd64_ref) * (1.0 / DA_HEAD_DIM) + RMS_EPS) * qw_ref[...]
    k = proj(C_DAK, DA_W)
    k = k * lax.rsqrt(_group_sumsq(k, bd64_ref) * (1.0 / DA_HEAD_DIM) + RMS_EPS) * kw_ref[...]
    tm = x.shape[0]
    lane = lax.broadcasted_iota(jnp.int32, (1, LANES), 1)
    pos = (pl.program_id(0) * tm + lax.broadcasted_iota(jnp.int32, (tm, 1), 0)) % seq
    pos = pos.astype(F32)
    for h in range(DA_HEADS):
        hs = slice(h * DA_V_DIM, (h + 1) * DA_V_DIM)
        b0 = pos * (LOG2E * 2.0 ** (-8.0 * (h + 1) / DA_HEADS))
        p0 = b0.astype(BF16).astype(F32)
        b1 = b0 - p0
        p1 = b1.astype(BF16).astype(F32)
        p2 = b1 - p1
        for c in range(2):
            own = (lane < DA_HEAD_DIM) if c == 0 else (lane >= DA_HEAD_DIM)
            first = DA_HEAD_DIM * (1 - c)
            cs = slice((2 * h + c) * LANES, (2 * h + c + 1) * LANES)
            ones = ((lane >= first) & (lane < first + 3)).astype(F32)
            daq_ref[:, cs] = jnp.where(own, q[:, hs], ones).astype(BF16)
            feat = jnp.where(lane == first, p0, jnp.where(lane == first + 1, p1, jnp.where(lane == first + 2, p2, 0.0)))
            dak_ref[:, cs] = jnp.where(own, k[:, hs], feat).astype(BF16)
    dav_ref[...] = proj(C_DAV, DA_W).astype(BF16)

    glq_ref[...] = proj(C_GLQ, GLA_QW) * (GLA_DK ** -0.5)
    glk_ref[...] = proj(C_GLK, GLA_QW)
    glv_ref[...] = proj(C_GLV, GLA_VW).astype(BF16)
    glvt_ref[...] = _dot_nt(wvt_ref[...], xn).astype(BF16)
    r = proj(C_GLR, GLA_VW)
    glr_ref[...] = (r * jax.nn.sigmoid(r)).astype(BF16)

    g = proj(C_GLG, LANES)
    z = jnp.dot(g, wg_ref[...], preferred_element_type=F32, precision=lax.Precision.HIGHEST) + bg_ref[...]
    log_sig = jnp.minimum(z, 0.0) - jnp.log1p(jnp.exp(-jnp.abs(z)))
    gla_ref[...] = log_sig * (1.0 / GLA_TAU)

    xq = proj(C_XAQ, XA_W)
    xaq_ref[...] = (xq * lax.rsqrt(_group_sumsq(xq, bd128_ref) * (1.0 / XA_HEAD_DIM) + RMS_EPS) * xw_ref[...]).astype(BF16)

    for c in range(0, GATE_W, 512):
        gate_ref[:, c:c + 512] = jax.nn.sigmoid(proj(C_GATE + c, 512)).astype(BF16)


def _block_diag_ones(width, group):
    idx = np.arange(width) // group
    return jnp.asarray((idx[:, None] == idx[None, :]).astype(np.float32), dtype=BF16)


def _in_proj(x2, mix_norm_w, w_in, da_q_norm_w, da_k_norm_w, xa_q_norm_w, gla_w_gate, gla_b_gate, *, seq, tm,
             interpret):
    n = x2.shape[0]
    c_glg_src = C_GLR + GLA_VW
    w = jnp.concatenate(
        [w_in[:, :c_glg_src], w_in[:, c_glg_src + GLA_GATE_RANK:], w_in[:, c_glg_src:c_glg_src + GLA_GATE_RANK],
         jnp.zeros((D_MODEL, LANES - GLA_GATE_RANK), w_in.dtype)], axis=1).astype(BF16)
    wvt = w_in[:, C_GLV:C_GLV + GLA_VW].T.astype(BF16)
    qw = jnp.tile(da_q_norm_w, 2 * DA_HEADS)[None, :] * (DA_HEAD_DIM ** -0.5 * LOG2E)
    kw = jnp.tile(da_k_norm_w, 2 * DA_HEADS)[None, :]
    xw = jnp.tile(xa_q_norm_w, XA_HEADS)[None, :] * (XA_HEAD_DIM ** -0.5)
    wg = jnp.zeros((LANES, GLA_QW), F32).at[:GLA_GATE_RANK].set(gla_w_gate)
    bg = gla_b_gate[None, :]
    row = lambda width: pl.BlockSpec((tm, width), lambda i: (i, 0))
    out_widths = [(2 * DA_W, BF16), (2 * DA_W, BF16), (DA_W, BF16), (GLA_QW, F32), (GLA_QW, F32), (GLA_VW, BF16)]
    out_shape = [jax.ShapeDtypeStruct((n, wd), dt) for wd, dt in out_widths]
    out_specs = [row(wd) for wd, _ in out_widths]
    out_shape.append(jax.ShapeDtypeStruct((GLA_VW, n), BF16))
    out_specs.append(pl.BlockSpec((GLA_VW, tm), lambda i: (0, i)))
    for wd, dt in [(GLA_VW, BF16), (GLA_QW, F32), (XA_W, BF16), (GATE_W, BF16)]:
        out_shape.append(jax.ShapeDtypeStruct((n, wd), dt))
        out_specs.append(row(wd))
    return pl.pallas_call(
        functools.partial(_in_proj_body, seq=seq),
        grid=(n // tm,),
        in_specs=[row(D_MODEL), _resident((1, D_MODEL)), _resident((D_MODEL, C_END)), _resident((GLA_VW, D_MODEL)),
                  _resident((1, DA_W)), _resident((1, DA_W)), _resident((1, XA_W)), _resident((LANES, GLA_QW)),
                  _resident((1, GLA_QW)), _resident((DA_W, DA_W)), _resident((XA_W, XA_W))],
        out_specs=out_specs,
        out_shape=out_shape,
        compiler_params=_cparams(("parallel",)),
        name="in_proj",
        interpret=interpret,
    )(x2, mix_norm_w[None, :], w, wvt, qw, kw, xw, wg, bg, _block_diag_ones(DA_W, DA_HEAD_DIM),
      _block_diag_ones(XA_W, XA_HEAD_DIM))


def _diff_attn_body(qi_tab, kj_tab, q_ref, k_ref, v_ref, lam_ref, ow_ref, o_ref, m_scr, acc_scr, *, blk_q, blk_k,
                    lam_init):
    p = pl.program_id(2)
    qi = qi_tab[p]
    kj = kj_tab[p]
    last = (qi * blk_q) // blk_k

    @pl.when(kj == 0)
    def _():
        m_scr[...] = jnp.full(m_scr.shape, -jnp.inf, F32)
        acc_scr[...] = jnp.zeros(acc_scr.shape, F32)

    v = v_ref[...]
    v_ones = jnp.concatenate([v, jnp.ones_like(v)], axis=1)

    def component(c, causal):
        cs = slice(c * LANES, (c + 1) * LANES)
        s = _dot_nt(q_ref[:, cs], k_ref[:, cs])
        if causal:
            row = lax.broadcasted_iota(jnp.int32, (blk_q, 1), 0) + qi * blk_q
            col = lax.broadcasted_iota(jnp.int32, (1, blk_k), 1) + kj * blk_k
            s = jnp.where(col <= row, s, -jnp.inf)
        m_prev = m_scr[c]
        m_new = jnp.maximum(m_prev, jnp.max(s, axis=-1, keepdims=True))
        acc_scr[c] = jnp.exp2(m_prev - m_new) * acc_scr[c] + _dot(jnp.exp2(s - m_new).astype(BF16), v_ones)
        m_scr[c] = m_new

    @pl.when(kj < last)
    def _():
        component(0, False)
        component(1, False)

    @pl.when(kj == last)
    def _():
        component(0, True)
        component(1, True)
        lv = lam_ref[...]
        lam = (jnp.exp(jnp.sum(lv[0:1] * lv[1:2], axis=-1, keepdims=True))
               - jnp.exp(jnp.sum(lv[2:3] * lv[3:4], axis=-1, keepdims=True)) + lam_init)
        a0 = acc_scr[0]
        a1 = acc_scr[1]
        o = a0[:, :DA_V_DIM] / a0[:, DA_V_DIM:] - lam * (a1[:, :DA_V_DIM] / a1[:, DA_V_DIM:])
        ms = jnp.mean(o * o, axis=-1, keepdims=True)
        o_ref[...] = (o * lax.rsqrt(ms + RMS_EPS) * ow_ref[...] * (1.0 - lam_init)).astype(BF16)


def _diff_attn(daq, dak, dav, da_lambda, da_out_norm_w, *, batch, seq, blk_q, blk_k, lam_init, interpret):
    assert blk_k % blk_q == 0 and seq % blk_k == 0
    nq = seq // blk_q
    nk = seq // blk_k
    pairs = [(qi, kj) for qi in range(nq) for kj in range((qi * blk_q) // blk_k + 1)]
    qi_tab = jnp.asarray([p[0] for p in pairs], jnp.int32)
    kj_tab = jnp.asarray([p[1] for p in pairs], jnp.int32)
    qmap = lambda b, h, p, qt, kt: (b * nq + qt[p], h)
    kmap = lambda b, h, p, qt, kt: (b * nk + kt[p], h)
    grid_spec = pltpu.PrefetchScalarGridSpec(
        num_scalar_prefetch=2,
        grid=(batch, DA_HEADS, len(pairs)),
        in_specs=[pl.BlockSpec((blk_q, 2 * LANES), qmap), pl.BlockSpec((blk_k, 2 * LANES), kmap),
                  pl.BlockSpec((blk_k, DA_V_DIM), kmap),
                  pl.BlockSpec((4, DA_HEAD_DIM), lambda b, h, p, qt, kt: (0, 0)),
                  pl.BlockSpec((1, DA_V_DIM), lambda b, h, p, qt, kt: (0, 0))],
        out_specs=pl.BlockSpec((blk_q, DA_V_DIM), qmap),
        scratch_shapes=[pltpu.VMEM((2, blk_q, 1), F32), pltpu.VMEM((2, blk_q, 2 * DA_V_DIM), F32)],
    )
    return pl.pallas_call(
        functools.partial(_diff_attn_body, blk_q=blk_q, blk_k=blk_k, lam_init=lam_init),
        grid_spec=grid_spec,
        out_shape=jax.ShapeDtypeStruct((batch * seq, DA_HEADS * DA_V_DIM), BF16),
        compiler_params=_cparams(("parallel", "parallel", "arbitrary")),
        name="diff_attn",
        interpret=interpret,
    )(qi_tab, kj_tab, daq, dak, dav, da_lambda, da_out_norm_w[None, :])


def _gla_constants():
    c = GLA_CHUNK
    t = np.arange(c)[:, None]
    u = np.arange(c)[None, :]
    lmats, masks = [], []
    for lev in range(GLA_LEVELS):
        m = 1 << lev
        second = (t % (2 * m)) >= m
        first = ~second
        bnd = (t // m) * m
        lmats.append(second & (u >= bnd) & (u <= t))
    for lev in range(GLA_LEVELS):
        m = 1 << lev
        first = (t % (2 * m)) < m
        end = (t // m) * m + m - 1
        lmats.append(first & (u > t) & (u <= end))
    lmats.append(u <= t)
    lmats.append(u > t)
    for lev in range(GLA_LEVELS):
        m = 1 << lev
        s = np.arange(c)[None, :]
        masks.append(((t // (2 * m)) == (s // (2 * m))) & ((t % (2 * m)) >= m) & ((s % (2 * m)) < m))
    masks.append(t == np.arange(c)[None, :])
    lall = jnp.asarray(np.concatenate(lmats, axis=0).astype(np.float32), dtype=BF16)
    mall = jnp.asarray(np.stack(masks, axis=0).astype(np.float32))
    return lall, mall


def _gla_body(q_ref, k_ref, la_ref, v_ref, vt_ref, r_ref, lall_ref, mall_ref, ow_ref, o_ref, state_scr):
    c = GLA_CHUNK

    @pl.when(pl.program_id(1) == 0)
    def _():
        state_scr[...] = jnp.zeros(state_scr.shape, F32)

    q = q_ref[...]
    k = k_ref[...]
    g = la_ref[...]
    g1 = g.astype(BF16)
    r1 = g - g1.astype(F32)
    g2 = r1.astype(BF16)
    g3 = (r1 - g2.astype(F32)).astype(BF16)
    lall = lall_ref[...]
    e_all = jnp.exp(_dot(lall, g1) + _dot(lall, g2) + _dot(lall, g3))

    def rows(i):
        return e_all[i * c:(i + 1) * c]

    lane = lax.broadcasted_iota(jnp.int32, (1, GLA_QW), 1)
    head_masks = [(lane // GLA_DK == h).astype(F32) for h in range(GLA_HEADS)]
    a = [jnp.zeros((c, c), F32) for _ in range(GLA_HEADS)]
    for lev in range(GLA_LEVELS + 1):
        if lev < GLA_LEVELS:
            ql = q * rows(lev)
            kl = (k * rows(GLA_LEVELS + lev)).astype(BF16)
        else:
            ql = q
            kl = k.astype(BF16)
        mask = mall_ref[lev]
        for h in range(GLA_HEADS):
            a[h] = a[h] + mask * _dot_nt((ql * head_masks[h]).astype(BF16), kl)

    e_b = rows(2 * GLA_LEVELS)
    q_dec = q * e_b
    k_dec = (k * rows(2 * GLA_LEVELS + 1)).astype(BF16)
    chunk_decay = e_b[c - 1:c, :]
    for h in range(GLA_HEADS):
        vs = slice(h * GLA_DV, (h + 1) * GLA_DV)
        st = state_scr[h]
        o = _dot(a[h].astype(BF16), v_ref[:, vs]) + _dot_nt((q_dec * head_masks[h]).astype(BF16), st.astype(BF16))
        state_scr[h] = st * chunk_decay + _dot(vt_ref[vs, :], k_dec)
        ms = jnp.mean(o * o, axis=-1, keepdims=True)
        o_ref[:, vs] = (o * lax.rsqrt(ms + RMS_EPS) * ow_ref[...] * r_ref[:, vs].astype(F32)).astype(BF16)


def _gla(glq, glk, gla, glv, glvt, glr, gla_out_norm_w, *, batch, seq, interpret):
    c = GLA_CHUNK
    nc = seq // c
    lall, mall = _gla_constants()
    row = lambda width: pl.BlockSpec((c, width), lambda b, i: (b * nc + i, 0))
    return pl.pallas_call(
        _gla_body,
        grid=(batch, nc),
        in_specs=[row(GLA_QW), row(GLA_QW), row(GLA_QW), row(GLA_VW),
                  pl.BlockSpec((GLA_VW, c), lambda b, i: (0, b * nc + i)), row(GLA_VW),
                  _resident(lall.shape), _resident(mall.shape), _resident((1, GLA_DV))],
        out_specs=row(GLA_VW),
        out_shape=jax.ShapeDtypeStruct((batch * seq, GLA_VW), BF16),
        scratch_shapes=[pltpu.VMEM((GLA_HEADS, GLA_DV, GLA_QW), F32)],
        compiler_params=_cparams(("parallel", "arbitrary")),
        name="gla",
        interpret=interpret,
    )(glq, glk, gla, glv, glvt, glr, lall, mall, gla_out_norm_w[None, :])


def _mem_kv_body(mem_ref, nw_ref, w_ref, kw_ref, bd_ref, mk_ref, mv_ref):
    x = mem_ref[...]
    ms = jnp.mean(x * x, axis=-1, keepdims=True)
    xn = (x * lax.rsqrt(ms + RMS_EPS) * nw_ref[...]).astype(BF16)
    k = _dot(xn, w_ref[:, :XA_W])
    mk_ref[...] = (k * lax.rsqrt(_group_sumsq(k, bd_ref) * (1.0 / XA_HEAD_DIM) + RMS_EPS) * kw_ref[...]).astype(BF16)
    mv_ref[...] = _dot(xn, w_ref[:, XA_W:]).astype(BF16)


def _mem_kv(mem2, mem_norm_w, w_mem_kv, xa_k_norm_w, *, batch, mem_len, interpret):
    blk = pl.BlockSpec((mem_len, XA_W), lambda b: (b, 0))
    return pl.pallas_call(
        _mem_kv_body,
        grid=(batch,),
        in_specs=[pl.BlockSpec((mem_len, D_MODEL), lambda b: (b, 0)), _resident((1, D_MODEL)),
                  _resident((D_MODEL, 2 * XA_W)), _resident((1, XA_W)), _resident((XA_W, XA_W))],
        out_specs=[blk, blk],
        out_shape=[jax.ShapeDtypeStruct((batch * mem_len, XA_W), BF16)] * 2,
        compiler_params=_cparams(("parallel",)),
        name="mem_kv",
        interpret=interpret,
    )(mem2, mem_norm_w[None, :], w_mem_kv.astype(BF16), jnp.tile(xa_k_norm_w, XA_HEADS)[None, :],
      _block_diag_ones(XA_W, XA_HEAD_DIM))


def _piece_offset(j):
    return (j % 2) * (D_MODEL // 2) + (j // 2) * LANES


def _merge_body(x_ref, oda_ref, ogla_ref, xaq_ref, gate_ref, mk_ref, mv_ref, wda_ref, wgla_ref, wxa_ref, wout_ref,
                fw_ref, wq_ref, keys_ref, x1_ref, x8_ref, sc_ref):
    tm = x_ref.shape[0]
    br_xa = None
    for h in range(XA_HEADS):
        hs = slice(h * XA_HEAD_DIM, (h + 1) * XA_HEAD_DIM)
        s = _dot_nt(xaq_ref[:, hs], mk_ref[:, hs])
        s = s - jnp.max(s, axis=-1, keepdims=True)
        p = jnp.exp(s)
        p = p / jnp.sum(p, axis=-1, keepdims=True)
        o = _dot(p.astype(BF16), mv_ref[:, hs])
        t = _dot(o.astype(BF16), wxa_ref[hs, :])
        br_xa = t if br_xa is None else br_xa + t
    merged = (gate_ref[:, 0:D_MODEL].astype(F32) * _dot(oda_ref[...], wda_ref[...])
              + gate_ref[:, D_MODEL:2 * D_MODEL].astype(F32) * _dot(ogla_ref[...], wgla_ref[...])
              + gate_ref[:, 2 * D_MODEL:].astype(F32) * br_xa)
    x1 = x_ref[...] + _dot(merged.astype(BF16), wout_ref[...])
    x1_ref[...] = x1
    ms = jnp.mean(x1 * x1, axis=-1, keepdims=True)
    xn = x1 * lax.rsqrt(ms + RMS_EPS) * fw_ref[...]
    xnb = xn.astype(BF16)
    for j in range(ROW_PIECES):
        off = _piece_offset(j)
        x8_ref[pl.ds(j, tm, stride=ROW_PIECES), :] = xn[:, off:off + LANES]
    pq = _dot(xnb, wq_ref[...])
    half = PEER_QDIM // 2
    for hp in range(2 * PEER_HEADS):
        q_hi, q_lo = _split2(pq[:, hp * half:(hp + 1) * half])
        keys = keys_ref[hp % 2]
        sc_ref[hp] = _dot_nt(keys, q_hi) + _dot_nt(keys, q_lo)


def _merge(x2, oda, ogla, xaq, gates, mk, mv, w_br_da, w_br_gla, w_br_xa, w_out, ffn_norm_w, peer_w_q, peer_sub_keys,
           *, seq, mem_len, tm, interpret):
    n = x2.shape[0]
    spb = seq // tm
    row = lambda width: pl.BlockSpec((tm, width), lambda i: (i, 0))
    mem_blk = pl.BlockSpec((mem_len, XA_W), lambda i: (i // spb, 0))
    nhp = 2 * PEER_HEADS
    return pl.pallas_call(
        _merge_body,
        grid=(n // tm,),
        in_specs=[row(D_MODEL), row(DA_W), row(GLA_VW), row(XA_W), row(GATE_W), mem_blk, mem_blk,
                  _resident((DA_W, D_MODEL)), _resident((GLA_VW, D_MODEL)), _resident((XA_W, D_MODEL)),
                  _resident((D_MODEL, D_MODEL)), _resident((1, D_MODEL)),
                  _resident((D_MODEL, PEER_HEADS * PEER_QDIM)), _resident((2, PEER_N_KEYS, PEER_QDIM // 2))],
        out_specs=[row(D_MODEL), pl.BlockSpec((tm * ROW_PIECES, LANES), lambda i: (i, 0)),
                   pl.BlockSpec((nhp, PEER_N_KEYS, tm), lambda i: (0, 0, i))],
        out_shape=[jax.ShapeDtypeStruct((n, D_MODEL), F32), jax.ShapeDtypeStruct((n * ROW_PIECES, LANES), F32),
                   jax.ShapeDtypeStruct((nhp, PEER_N_KEYS, n), F32)],
        compiler_params=_cparams(("parallel",)),
        name="merge",
        interpret=interpret,
    )(x2, oda, ogla, xaq, gates, mk, mv, w_br_da.astype(BF16), w_br_gla.astype(BF16), w_br_xa.astype(BF16),
      w_out.astype(BF16), ffn_norm_w[None, :], peer_w_q.astype(BF16), peer_sub_keys.astype(BF16))


def _top16(x, code):
    vals, codes = [], []
    for _ in range(PEER_TOPK):
        m = jnp.max(x, axis=0, keepdims=True)
        win = jnp.min(jnp.where(x == m, code, jnp.inf), axis=0, keepdims=True)
        vals.append(m)
        codes.append(win)
        x = jnp.where(code == win, -jnp.inf, x)
    return jnp.concatenate(vals, axis=0), jnp.concatenate(codes, axis=0).astype(jnp.int32)


def _select_rows(table, pos):
    out = jnp.zeros(pos.shape, table.dtype)
    for i in range(PEER_TOPK):
        out = jnp.where(pos == i, table[i:i + 1, :], out)
    return out


def _pair_candidates():
    blocks = [(0, 1, 0, PEER_TOPK)] + [(i, i + 1, 0, SUBLANES) for i in range(1, SUBLANES)]
    blocks.append((SUBLANES, PEER_TOPK, 0, 1))
    codes = np.concatenate([np.array([i * PEER_TOPK + j for i in range(i0, i1) for j in range(j0, j1)])
                            for i0, i1, j0, j1 in blocks])
    return blocks, codes.astype(np.float32)[:, None]


def _peer_topk_body(sc_ref, pair_code_ref, idx_ref, g_ref):
    blocks, _ = _pair_candidates()
    tb = idx_ref.shape[1]
    key_code = lax.broadcasted_iota(jnp.int32, (PEER_N_KEYS, tb), 0).astype(F32)
    pair_code = jnp.broadcast_to(pair_code_ref[...], (pair_code_ref.shape[0], tb))
    experts, gates = [], []
    for h in range(PEER_HEADS):
        s0, i0 = _top16(sc_ref[2 * h], key_code)
        s1, i1 = _top16(sc_ref[2 * h + 1], key_code)
        cand = jnp.concatenate([s0[a0:a1, :] + s1[b0:b1, :] for a0, a1, b0, b1 in blocks], axis=0)
        best, flat = _top16(cand, pair_code)
        expert = (_select_rows(i0, jnp.right_shift(flat, 4)) * PEER_N_KEYS
                  + _select_rows(i1, jnp.bitwise_and(flat, PEER_TOPK - 1)))
        e = jnp.exp(best - best[0:1, :])
        experts.append(expert * ROW_SUBLANES)
        gates.append(e / jnp.sum(e, axis=0, keepdims=True))
    idx_ref[...] = jnp.concatenate(experts, axis=0)
    g_ref[...] = jnp.concatenate(gates, axis=0).T


def _peer_topk(scores, *, tb, interpret):
    nhp, nk, n = scores.shape
    pair_code = jnp.asarray(_pair_candidates()[1])
    return pl.pallas_call(
        _peer_topk_body,
        grid=(n // tb,),
        in_specs=[pl.BlockSpec((nhp, nk, tb), lambda i: (0, 0, i)), _resident(pair_code.shape)],
        out_specs=[pl.BlockSpec((PEER_SLOTS, tb), lambda i: (0, i)), pl.BlockSpec((tb, PEER_SLOTS), lambda i: (i, 0))],
        out_shape=[jax.ShapeDtypeStruct((PEER_SLOTS, n), jnp.int32), jax.ShapeDtypeStruct((n, PEER_SLOTS), F32)],
        compiler_params=_cparams(("parallel",)),
        name="peer_topk",
        interpret=interpret,
    )(scores, pair_code)


def _pack_table(t):
    e = t.shape[0]
    bits = lax.bitcast_convert_type(t.astype(BF16), jnp.uint16).astype(jnp.uint32)
    bits = bits.reshape(e, 2, ROW_SUBLANES, LANES)
    packed = (bits[:, 0] | (bits[:, 1] << 16)).reshape(e * ROW_SUBLANES, LANES)
    return jnp.concatenate([packed, jnp.zeros((SUBLANES, LANES), jnp.uint32)], axis=0)


def _gather_pair(idx_scr, tab_ref, g_scr, t, slot):
    for d in range(2):
        for k in range(PEER_SLOTS):
            off = pl.multiple_of(idx_scr.at[k][t + d], ROW_SUBLANES)
            g_scr[slot, k * ROW_SUBLANES:(k + 1) * ROW_SUBLANES, d * LANES:(d + 1) * LANES] = (
                tab_ref[pl.ds(off, ROW_SUBLANES), :])


def _token_loop(idx_scr, tab_ref, g_scr, compute_pair, tb, pairs_per_step, base):
    _gather_pair(idx_scr, tab_ref, g_scr, 0, 0)

    def body(i, carry):
        t0 = 2 * pairs_per_step * i
        for p in range(pairs_per_step):
            t = t0 + 2 * p
            _gather_pair(idx_scr, tab_ref, g_scr, jnp.minimum(t + 2, tb - 2), (p + 1) % 2)
            compute_pair(base + t, p % 2)
        return carry

    lax.fori_loop(0, tb // (2 * pairs_per_step), body, 0)


def _two_block_loop(idx_hbm, idx_a, idx_b, sem, run_block, tb):
    i = pl.program_id(0)

    def copy(blk, scr, s):
        return pltpu.make_async_copy(idx_hbm.at[:, pl.ds(pl.multiple_of(blk * tb, LANES), tb)], scr, sem.at[s])

    @pl.when(i == 0)
    def _():
        copy(0, idx_a, 0).start()

    copy(2 * i, idx_a, 0).wait()
    copy(2 * i + 1, idx_b, 1).start()
    run_block(idx_a, 0)
    copy(2 * i + 1, idx_b, 1).wait()

    @pl.when(i + 1 < pl.num_programs(0))
    def _():
        copy(2 * i + 2, idx_a, 0).start()

    run_block(idx_b, tb)


def _piece_mask(period):
    r = lax.broadcasted_iota(jnp.int32, (ROW_PIECES, PEER_SLOTS * period), 0)
    c = lax.broadcasted_iota(jnp.int32, (ROW_PIECES, PEER_SLOTS * period), 1)
    return (c % period == r).astype(F32)


def _peer_u_body(idx_hbm, x8_ref, g_ref, tab_ref, sel_ref, selt_ref, w_ref, idx_a, idx_b, sem, g_scr, r_scr, *, tb):
    dmask = _piece_mask(ROW_PIECES)
    first = lax.broadcasted_iota(jnp.int32, (2 * ROW_PIECES, LANES), 0) < ROW_PIECES

    def compute_pair(t, slot):
        rows = pltpu.bitcast(g_scr[slot], BF16)
        x16 = x8_ref[pl.ds(pl.multiple_of(t * ROW_PIECES, 2 * ROW_PIECES), 2 * ROW_PIECES), :]
        lhs = jnp.concatenate([jnp.where(first, x16, 0.0), jnp.where(first, 0.0, x16)], axis=1).astype(BF16)
        r = _dot_nt(lhs, rows)
        r_scr[pl.ds(t, 1), :] = jnp.sum(r[:ROW_PIECES] * dmask, axis=0, keepdims=True)
        r_scr[pl.ds(t + 1, 1), :] = jnp.sum(r[ROW_PIECES:] * dmask, axis=0, keepdims=True)

    _two_block_loop(idx_hbm, idx_a, idx_b, sem,
                    lambda idx_scr, base: _token_loop(idx_scr, tab_ref, g_scr, compute_pair, tb, 16, base), tb)
    hi, lo = _split2(r_scr[...])
    s = _dot(hi, sel_ref[...]) + _dot(lo, sel_ref[...])
    act = 0.5 * s * (1.0 + lax.erf(s * (2.0 ** -0.5)))
    w = (g_ref[...] * act).astype(BF16)
    w_ref[...] = _dot(w, selt_ref[...])


def _peer_v_body(idx_hbm, w_ref, x1_ref, tab_ref, o_ref, idx_a, idx_b, sem, o8_scr, *, tb):
    dmask = _piece_mask(2 * ROW_PIECES)

    def run_block(idx_scr, base):
        def body(i, carry):
            for u in range(V_TOKENS_PER_STEP):
                t = V_TOKENS_PER_STEP * i + u
                tiles = [tab_ref[pl.ds(pl.multiple_of(idx_scr.at[k][t], ROW_SUBLANES), SUBLANES), :]
                         for k in range(PEER_SLOTS)]
                rows = pltpu.bitcast(jnp.concatenate(tiles, axis=0), BF16)
                lhs = (w_ref[pl.ds(base + t, 1), :] * dmask).astype(BF16)
                o8_scr[pl.ds(pl.multiple_of((base + t) * ROW_PIECES, ROW_PIECES), ROW_PIECES), :] = _dot(lhs, rows)
            return carry

        lax.fori_loop(0, tb // V_TOKENS_PER_STEP, body, 0)

    _two_block_loop(idx_hbm, idx_a, idx_b, sem, run_block, tb)
    for j in range(ROW_PIECES):
        cs = slice(_piece_offset(j), _piece_offset(j) + LANES)
        o_ref[:, cs] = x1_ref[:, cs] + o8_scr[pl.ds(j, 2 * tb, stride=ROW_PIECES), :]


def _idx_scratch(tb):
    return [pltpu.SMEM((PEER_SLOTS, tb), jnp.int32), pltpu.SMEM((PEER_SLOTS, tb), jnp.int32),
            pltpu.SemaphoreType.DMA((2,))]


def _peer_u(idx_t, x8, gates, table, *, tb, interpret):
    n = idx_t.shape[1]
    ts = 2 * tb
    sel_np = (np.arange(PEER_SLOTS * ROW_PIECES)[:, None] // ROW_PIECES == np.arange(PEER_SLOTS)[None, :])
    sel = jnp.asarray(sel_np.astype(np.float32), dtype=BF16)
    col = np.arange(PEER_SLOTS * 2 * ROW_PIECES)[None, :]
    selt_np = (col // (2 * ROW_PIECES) == np.arange(PEER_SLOTS)[:, None]) & (col % (2 * ROW_PIECES) < ROW_PIECES)
    selt = jnp.asarray(selt_np.astype(np.float32), dtype=BF16)
    return pl.pallas_call(
        functools.partial(_peer_u_body, tb=tb),
        grid=(n // ts,),
        in_specs=[pl.BlockSpec(memory_space=pl.ANY),
                  pl.BlockSpec((ts * ROW_PIECES, LANES), lambda i: (i, 0)),
                  pl.BlockSpec((ts, PEER_SLOTS), lambda i: (i, 0)),
                  _resident(table.shape), _resident(sel.shape), _resident(selt.shape)],
        out_specs=pl.BlockSpec((ts, PEER_SLOTS * 2 * ROW_PIECES), lambda i: (i, 0)),
        out_shape=jax.ShapeDtypeStruct((n, PEER_SLOTS * 2 * ROW_PIECES), F32),
        scratch_shapes=_idx_scratch(tb) + [pltpu.VMEM((2, PEER_SLOTS * ROW_SUBLANES, 2 * LANES), jnp.uint32),
                                           pltpu.VMEM((ts, PEER_SLOTS * ROW_PIECES), F32)],
        compiler_params=_cparams(("arbitrary",)),
        name="peer_u",
        interpret=interpret,
    )(idx_t, x8, gates, table, sel, selt)


def _peer_v(idx_t, w_exp, x1, table, *, tb, interpret):
    n = idx_t.shape[1]
    ts = 2 * tb
    return pl.pallas_call(
        functools.partial(_peer_v_body, tb=tb),
        grid=(n // ts,),
        in_specs=[pl.BlockSpec(memory_space=pl.ANY),
                  pl.BlockSpec((ts, PEER_SLOTS * 2 * ROW_PIECES), lambda i: (i, 0)),
                  pl.BlockSpec((ts, D_MODEL), lambda i: (i, 0)),
                  _resident(table.shape)],
        out_specs=pl.BlockSpec((ts, D_MODEL), lambda i: (i, 0)),
        out_shape=jax.ShapeDtypeStruct((n, D_MODEL), F32),
        scratch_shapes=_idx_scratch(tb) + [pltpu.VMEM((ts * ROW_PIECES, LANES), F32)],
        compiler_params=_cparams(("arbitrary",)),
        name="peer_v",
        interpret=interpret,
    )(idx_t, w_exp, x1, table)


def _layer(x2, mem2, p, *, layer, batch, seq, mem_len, tm, attn_blk_q, attn_blk_k, peer_tb, interpret):
    lam_init = 0.8 - 0.6 * math.exp(-0.3 * layer)
    (daq, dak, dav, glq, glk, glv, glvt, glr, gla, xaq, gates) = _in_proj(
        x2, p["mix_norm_w"], p["w_in"], p["da_q_norm_w"], p["da_k_norm_w"], p["xa_q_norm_w"], p["gla_w_gate"],
        p["gla_b_gate"], seq=seq, tm=tm, interpret=interpret)
    oda = _diff_attn(daq, dak, dav, p["da_lambda"], p["da_out_norm_w"], batch=batch, seq=seq, blk_q=attn_blk_q,
                     blk_k=attn_blk_k, lam_init=lam_init, interpret=interpret)
    ogla = _gla(glq, glk, gla, glv, glvt, glr, p["gla_out_norm_w"], batch=batch, seq=seq, interpret=interpret)
    mk, mv = _mem_kv(mem2, p["mem_norm_w"], p["w_mem_kv"], p["xa_k_norm_w"], batch=batch, mem_len=mem_len,
                     interpret=interpret)
    x1, x8, scores = _merge(x2, oda, ogla, xaq, gates, mk, mv, p["w_br_da"], p["w_br_gla"], p["w_br_xa"], p["w_out"],
                             p["ffn_norm_w"], p["peer_w_q"], p["peer_sub_keys"], seq=seq, mem_len=mem_len, tm=tm,
                             interpret=interpret)
    idx_t, peer_gates = _peer_topk(scores, tb=peer_tb, interpret=interpret)
    w_exp = _peer_u(idx_t, x8, peer_gates, _pack_table(p["peer_u"]), tb=peer_tb, interpret=interpret)
    return _peer_v(idx_t, w_exp, x1, _pack_table(p["peer_v"]), tb=peer_tb, interpret=interpret)


_PARAM_NAMES = ("mix_norm_w", "w_in", "da_q_norm_w", "da_k_norm_w", "da_lambda", "da_out_norm_w", "gla_w_gate",
                "gla_b_gate", "gla_out_norm_w", "mem_norm_w", "w_mem_kv", "xa_q_norm_w", "xa_k_norm_w", "w_br_da",
                "w_br_gla", "w_br_xa", "w_out", "ffn_norm_w", "peer_w_q", "peer_sub_keys", "peer_u", "peer_v")


def _forward(x, mem, params, *, tm=256, attn_blk_q=512, attn_blk_k=2048, peer_tb=128, interpret=False):
    batch, seq, d = x.shape
    mem_len = mem.shape[1]
    x2 = x.reshape(batch * seq, d)
    mem2 = mem.reshape(batch * mem_len, d)
    depth = params["w_in"].shape[0]
    for layer in range(depth):
        p = {name: params[name][layer] for name in _PARAM_NAMES}
        x2 = _layer(x2, mem2, p, layer=layer, batch=batch, seq=seq, mem_len=mem_len, tm=tm, attn_blk_q=attn_blk_q,
                    attn_blk_k=attn_blk_k, peer_tb=peer_tb, interpret=interpret)
    return x2.reshape(batch, seq, d)


def kernel(x, mem, mix_norm_w, w_in, da_q_norm_w, da_k_norm_w, da_lambda, da_out_norm_w, gla_w_gate, gla_b_gate,
           gla_out_norm_w, mem_norm_w, w_mem_kv, xa_q_norm_w, xa_k_norm_w, w_br_da, w_br_gla, w_br_xa, w_out,
           ffn_norm_w, peer_w_q, peer_sub_keys, peer_u, peer_v):
    params = dict(zip(_PARAM_NAMES, (mix_norm_w, w_in, da_q_norm_w, da_k_norm_w, da_lambda, da_out_norm_w, gla_w_gate,
                                     gla_b_gate, gla_out_norm_w, mem_norm_w, w_mem_kv, xa_q_norm_w, xa_k_norm_w,
                                     w_br_da, w_br_gla, w_br_xa, w_out, ffn_norm_w, peer_w_q, peer_sub_keys, peer_u,
                                     peer_v)))
    return _forward(x, mem, params)
```

```python
import functools
import math

import jax
import jax.numpy as jnp
import numpy as np
from jax import lax
from jax.experimental import pallas as pl
from jax.experimental.pallas import tpu as pltpu

F32 = jnp.float32
BF16 = jnp.bfloat16

LANES = 128
SUBLANES = 8
VMEM_LIMIT_BYTES = 56 * 1024 * 1024

D_MODEL = 1024
DA_HEADS = 4
DA_HEAD_DIM = 64
DA_V_DIM = 2 * DA_HEAD_DIM
GLA_HEADS = 4
GLA_DK = 64
GLA_DV = 128
GLA_GATE_RANK = 16
GLA_TAU = 16.0
XA_HEADS = 4
XA_HEAD_DIM = 128
N_BRANCH = 3
PEER_HEADS = 8
PEER_N_KEYS = 128
PEER_QDIM = 256
PEER_TOPK = 16
RMS_EPS = 1e-6
LOG2E = math.log2(math.e)

DA_W = DA_HEADS * 2 * DA_HEAD_DIM
GLA_QW = GLA_HEADS * GLA_DK
GLA_VW = GLA_HEADS * GLA_DV
XA_W = XA_HEADS * XA_HEAD_DIM
GATE_W = N_BRANCH * D_MODEL
PEER_SLOTS = PEER_HEADS * PEER_TOPK

C_DAQ = 0
C_DAK = C_DAQ + DA_W
C_DAV = C_DAK + DA_W
C_GLQ = C_DAV + DA_W
C_GLK = C_GLQ + GLA_QW
C_GLV = C_GLK + GLA_QW
C_GLR = C_GLV + GLA_VW
C_XAQ = C_GLR + GLA_VW
C_GATE = C_XAQ + XA_W
C_GLG = C_GATE + GATE_W
C_END = C_GLG + LANES

GLA_CHUNK = 128
GLA_LEVELS = 7

ROW_SUBLANES = 4
ROW_PIECES = 2 * ROW_SUBLANES
PEER_TOKENS_PER_STEP = 16


def _cparams(sem):
    return pltpu.CompilerParams(dimension_semantics=sem, vmem_limit_bytes=VMEM_LIMIT_BYTES)


def _resident(shape):
    nd = len(shape)
    return pl.BlockSpec(shape, lambda *_: (0,) * nd, pipeline_mode=pl.Buffered(1))


def _dot(a, b):
    return jnp.dot(a, b, preferred_element_type=F32)


def _dot_nt(a, b):
    return lax.dot_general(a, b, (((1,), (1,)), ((), ())), preferred_element_type=F32)


def _split2(t):
    hi = t.astype(BF16)
    lo = (t - hi.astype(F32)).astype(BF16)
    return hi, lo


def _group_sumsq(t, bd_ref):
    hi, lo = _split2(t * t)
    bd = bd_ref[...]
    return _dot(hi, bd) + _dot(lo, bd)


def _in_proj_body(x_ref, nw_ref, w_ref, wvt_ref, qw_ref, kw_ref, xw_ref, wg_ref, bg_ref, bd64_ref, bd128_ref,
                  daq_ref, dak_ref, dav_ref, glq_ref, glk_ref, glv_ref, glvt_ref, glr_ref, gla_ref, xaq_ref,
                  gate_ref, *, seq):
    x = x_ref[...]
    ms = jnp.mean(x * x, axis=-1, keepdims=True)
    xn = (x * lax.rsqrt(ms + RMS_EPS) * nw_ref[...]).astype(BF16)

    def proj(c0, width):
        return _dot(xn, w_ref[:, c0:c0 + width])

    q = proj(C_DAQ, DA_W)
    q = q * lax.rsqrt(_group_sumsq(q, bd64_ref) * (1.0 / DA_HEAD_DIM) + RMS_EPS) * qw_ref[...]
    k = proj(C_DAK, DA_W)
    k = k * lax.rsqrt(_group_sumsq(k, bd64_ref) * (1.0 / DA_HEAD_DIM) + RMS_EPS) * kw_ref[...]
    tm = x.shape[0]
    lane = lax.broadcasted_iota(jnp.int32, (1, LANES), 1)
    pos = (pl.program_id(0) * tm + lax.broadcasted_iota(jnp.int32, (tm, 1), 0)) % seq
    pos = pos.astype(F32)
    for h in range(DA_HEADS):
        hs = slice(h * DA_V_DIM, (h + 1) * DA_V_DIM)
        b0 = pos * (LOG2E * 2.0 ** (-8.0 * (h + 1) / DA_HEADS))
        p0 = b0.astype(BF16).astype(F32)
        b1 = b0 - p0
        p1 = b1.astype(BF16).astype(F32)
        p2 = b1 - p1
        for c in range(2):
            own = (lane < DA_HEAD_DIM) if c == 0 else (lane >= DA_HEAD_DIM)
            first = DA_HEAD_DIM * (1 - c)
            cs = slice((2 * h + c) * LANES, (2 * h + c + 1) * LANES)
            ones = ((lane >= first) & (lane < first + 3)).astype(F32)
            daq_ref[:, cs] = jnp.where(own, q[:, hs], ones).astype(BF16)
            feat = jnp.where(lane == first, p0, jnp.where(lane == first + 1, p1, jnp.where(lane == first + 2, p2, 0.0)))
            dak_ref[:, cs] = jnp.where(own, k[:, hs], feat).astype(BF16)
    dav_ref[...] = proj(C_DAV, DA_W).astype(BF16)

    glq_ref[...] = proj(C_GLQ, GLA_QW) * (GLA_DK ** -0.5)
    glk_ref[...] = proj(C_GLK, GLA_QW)
    glv_ref[...] = proj(C_GLV, GLA_VW).astype(BF16)
    glvt_ref[...] = _dot_nt(wvt_ref[...], xn).astype(BF16)
    r = proj(C_GLR, GLA_VW)
    glr_ref[...] = (r * jax.nn.sigmoid(r)).astype(BF16)

    g = proj(C_GLG, LANES)
    z = jnp.dot(g, wg_ref[...], preferred_element_type=F32, precision=lax.Precision.HIGHEST) + bg_ref[...]
    log_sig = jnp.minimum(z, 0.0) - jnp.log1p(jnp.exp(-jnp.abs(z)))
    gla_ref[...] = log_sig * (1.0 / GLA_TAU)

    xq = proj(C_XAQ, XA_W)
    xaq_ref[...] = (xq * lax.rsqrt(_group_sumsq(xq, bd128_ref) * (1.0 / XA_HEAD_DIM) + RMS_EPS) * xw_ref[...]).astype(BF16)

    for c in range(0, GATE_W, 512):
        gate_ref[:, c:c + 512] = jax.nn.sigmoid(proj(C_GATE + c, 512)).astype(BF16)


def _block_diag_ones(width, group):
    idx = np.arange(width) // group
    return jnp.asarray((idx[:, None] == idx[None, :]).astype(np.float32), dtype=BF16)


def _in_proj(x2, mix_norm_w, w_in, da_q_norm_w, da_k_norm_w, xa_q_norm_w, gla_w_gate, gla_b_gate, *, seq, tm,
             interpret):
    n = x2.shape[0]
    c_glg_src = C_GLR + GLA_VW
    w = jnp.concatenate(
        [w_in[:, :c_glg_src], w_in[:, c_glg_src + GLA_GATE_RANK:], w_in[:, c_glg_src:c_glg_src + GLA_GATE_RANK],
         jnp.zeros((D_MODEL, LANES - GLA_GATE_RANK), w_in.dtype)], axis=1).astype(BF16)
    wvt = w_in[:, C_GLV:C_GLV + GLA_VW].T.astype(BF16)
    qw = jnp.tile(da_q_norm_w, 2 * DA_HEADS)[None, :] * (DA_HEAD_DIM ** -0.5 * LOG2E)
    kw = jnp.tile(da_k_norm_w, 2 * DA_HEADS)[None, :]
    xw = jnp.tile(xa_q_norm_w, XA_HEADS)[None, :] * (XA_HEAD_DIM ** -0.5)
    wg = jnp.zeros((LANES, GLA_QW), F32).at[:GLA_GATE_RANK].set(gla_w_gate)
    bg = gla_b_gate[None, :]
    row = lambda width: pl.BlockSpec((tm, width), lambda i: (i, 0))
    out_widths = [(2 * DA_W, BF16), (2 * DA_W, BF16), (DA_W, BF16), (GLA_QW, F32), (GLA_QW, F32), (GLA_VW, BF16)]
    out_shape = [jax.ShapeDtypeStruct((n, wd), dt) for wd, dt in out_widths]
    out_specs = [row(wd) for wd, _ in out_widths]
    out_shape.append(jax.ShapeDtypeStruct((GLA_VW, n), BF16))
    out_specs.append(pl.BlockSpec((GLA_VW, tm), lambda i: (0, i)))
    for wd, dt in [(GLA_VW, BF16), (GLA_QW, F32), (XA_W, BF16), (GATE_W, BF16)]:
        out_shape.append(jax.ShapeDtypeStruct((n, wd), dt))
        out_specs.append(row(wd))
    return pl.pallas_call(
        functools.partial(_in_proj_body, seq=seq),
        grid=(n // tm,),
        in_specs=[row(D_MODEL), _resident((1, D_MODEL)), _resident((D_MODEL, C_END)), _resident((GLA_VW, D_MODEL)),
                  _resident((1, DA_W)), _resident((1, DA_W)), _resident((1, XA_W)), _resident((LANES, GLA_QW)),
                  _resident((1, GLA_QW)), _resident((DA_W, DA_W)), _resident((XA_W, XA_W))],
        out_specs=out_specs,
        out_shape=out_shape,
        compiler_params=_cparams(("parallel",)),
        name="in_proj",
        interpret=interpret,
    )(x2, mix_norm_w[None, :], w, wvt, qw, kw, xw, wg, bg, _block_diag_ones(DA_W, DA_HEAD_DIM),
      _block_diag_ones(XA_W, XA_HEAD_DIM))


def _diff_attn_body(qi_tab, kj_tab, q_ref, k_ref, v_ref, lam_ref, ow_ref, o_ref, m_scr, acc_scr, *, blk_q, blk_k,
                    lam_init):
    p = pl.program_id(2)
    qi = qi_tab[p]
    kj = kj_tab[p]
    last = (qi * blk_q) // blk_k

    @pl.when(kj == 0)
    def _():
        m_scr[...] = jnp.full(m_scr.shape, -jnp.inf, F32)
        acc_scr[...] = jnp.zeros(acc_scr.shape, F32)

    v = v_ref[...]
    v_ones = jnp.concatenate([v, jnp.ones_like(v)], axis=1)

    def component(c, causal):
        cs = slice(c * LANES, (c + 1) * LANES)
        s = _dot_nt(q_ref[:, cs], k_ref[:, cs])
        if causal:
            row = lax.broadcasted_iota(jnp.int32, (blk_q, 1), 0) + qi * blk_q
            col = lax.broadcasted_iota(jnp.int32, (1, blk_k), 1) + kj * blk_k
            s = jnp.where(col <= row, s, -jnp.inf)
        m_prev = m_scr[c]
        m_new = jnp.maximum(m_prev, jnp.max(s, axis=-1, keepdims=True))
        acc_scr[c] = jnp.exp2(m_prev - m_new) * acc_scr[c] + _dot(jnp.exp2(s - m_new).astype(BF16), v_ones)
        m_scr[c] = m_new

    @pl.when(kj < last)
    def _():
        component(0, False)
        component(1, False)

    @pl.when(kj == last)
    def _():
        component(0, True)
        component(1, True)
        lv = lam_ref[...]
        lam = (jnp.exp(jnp.sum(lv[0:1] * lv[1:2], axis=-1, keepdims=True))
               - jnp.exp(jnp.sum(lv[2:3] * lv[3:4], axis=-1, keepdims=True)) + lam_init)
        a0 = acc_scr[0]
        a1 = acc_scr[1]
        o = a0[:, :DA_V_DIM] / a0[:, DA_V_DIM:] - lam * (a1[:, :DA_V_DIM] / a1[:, DA_V_DIM:])
        ms = jnp.mean(o * o, axis=-1, keepdims=True)
        o_ref[...] = (o * lax.rsqrt(ms + RMS_EPS) * ow_ref[...] * (1.0 - lam_init)).astype(BF16)


def _diff_attn(daq, dak, dav, da_lambda, da_out_norm_w, *, batch, seq, blk_q, blk_k, lam_init, interpret):
    assert blk_k % blk_q == 0 and seq % blk_k == 0
    nq = seq // blk_q
    nk = seq // blk_k
    pairs = [(qi, kj) for qi in range(nq) for kj in range((qi * blk_q) // blk_k + 1)]
    qi_tab = jnp.asarray([p[0] for p in pairs], jnp.int32)
    kj_tab = jnp.asarray([p[1] for p in pairs], jnp.int32)
    qmap = lambda b, h, p, qt, kt: (b * nq + qt[p], h)
    kmap = lambda b, h, p, qt, kt: (b * nk + kt[p], h)
    grid_spec = pltpu.PrefetchScalarGridSpec(
        num_scalar_prefetch=2,
        grid=(batch, DA_HEADS, len(pairs)),
        in_specs=[pl.BlockSpec((blk_q, 2 * LANES), qmap), pl.BlockSpec((blk_k, 2 * LANES), kmap),
                  pl.BlockSpec((blk_k, DA_V_DIM), kmap),
                  pl.BlockSpec((4, DA_HEAD_DIM), lambda b, h, p, qt, kt: (0, 0)),
                  pl.BlockSpec((1, DA_V_DIM), lambda b, h, p, qt, kt: (0, 0))],
        out_specs=pl.BlockSpec((blk_q, DA_V_DIM), qmap),
        scratch_shapes=[pltpu.VMEM((2, blk_q, 1), F32), pltpu.VMEM((2, blk_q, 2 * DA_V_DIM), F32)],
    )
    return pl.pallas_call(
        functools.partial(_diff_attn_body, blk_q=blk_q, blk_k=blk_k, lam_init=lam_init),
        grid_spec=grid_spec,
        out_shape=jax.ShapeDtypeStruct((batch * seq, DA_HEADS * DA_V_DIM), BF16),
        compiler_params=_cparams(("parallel", "parallel", "arbitrary")),
        name="diff_attn",
        interpret=interpret,
    )(qi_tab, kj_tab, daq, dak, dav, da_lambda, da_out_norm_w[None, :])


def _gla_constants():
    c = GLA_CHUNK
    t = np.arange(c)[:, None]
    u = np.arange(c)[None, :]
    lmats, masks = [], []
    for lev in range(GLA_LEVELS):
        m = 1 << lev
        second = (t % (2 * m)) >= m
        first = ~second
        bnd = (t // m) * m
        lmats.append(second & (u >= bnd) & (u <= t))
    for lev in range(GLA_LEVELS):
        m = 1 << lev
        first = (t % (2 * m)) < m
        end = (t // m) * m + m - 1
        lmats.append(first & (u > t) & (u <= end))
    lmats.append(u <= t)
    lmats.append(u > t)
    for lev in range(GLA_LEVELS):
        m = 1 << lev
        s = np.arange(c)[None, :]
        masks.append(((t // (2 * m)) == (s // (2 * m))) & ((t % (2 * m)) >= m) & ((s % (2 * m)) < m))
    masks.append(t == np.arange(c)[None, :])
    lall = jnp.asarray(np.concatenate(lmats, axis=0).astype(np.float32), dtype=BF16)
    mall = jnp.asarray(np.stack(masks, axis=0).astype(np.float32))
    return lall, mall


def _gla_body(q_ref, k_ref, la_ref, v_ref, vt_ref, r_ref, lall_ref, mall_ref, ow_ref, o_ref, state_scr):
    c = GLA_CHUNK

    @pl.when(pl.program_id(1) == 0)
    def _():
        state_scr[...] = jnp.zeros(state_scr.shape, F32)

    q = q_ref[...]
    k = k_ref[...]
    g = la_ref[...]
    g1 = g.astype(BF16)
    r1 = g - g1.astype(F32)
    g2 = r1.astype(BF16)
    g3 = (r1 - g2.astype(F32)).astype(BF16)
    lall = lall_ref[...]
    e_all = jnp.exp(_dot(lall, g1) + _dot(lall, g2) + _dot(lall, g3))

    def rows(i):
        return e_all[i * c:(i + 1) * c]

    lane = lax.broadcasted_iota(jnp.int32, (1, GLA_QW), 1)
    head_masks = [(lane // GLA_DK == h).astype(F32) for h in range(GLA_HEADS)]
    a = [jnp.zeros((c, c), F32) for _ in range(GLA_HEADS)]
    for lev in range(GLA_LEVELS + 1):
        if lev < GLA_LEVELS:
            ql = q * rows(lev)
            kl = (k * rows(GLA_LEVELS + lev)).astype(BF16)
        else:
            ql = q
            kl = k.astype(BF16)
        mask = mall_ref[lev]
        for h in range(GLA_HEADS):
            a[h] = a[h] + mask * _dot_nt((ql * head_masks[h]).astype(BF16), kl)

    e_b = rows(2 * GLA_LEVELS)
    q_dec = q * e_b
    k_dec = (k * rows(2 * GLA_LEVELS + 1)).astype(BF16)
    chunk_decay = e_b[c - 1:c, :]
    for h in range(GLA_HEADS):
        vs = slice(h * GLA_DV, (h + 1) * GLA_DV)
        st = state_scr[h]
        o = _dot(a[h].astype(BF16), v_ref[:, vs]) + _dot_nt((q_dec * head_masks[h]).astype(BF16), st.astype(BF16))
        state_scr[h] = st * chunk_decay + _dot(vt_ref[vs, :], k_dec)
        ms = jnp.mean(o * o, axis=-1, keepdims=True)
        o_ref[:, vs] = (o * lax.rsqrt(ms + RMS_EPS) * ow_ref[...] * r_ref[:, vs].astype(F32)).astype(BF16)


def _gla(glq, glk, gla, glv, glvt, glr, gla_out_norm_w, *, batch, seq, interpret):
    c = GLA_CHUNK
    nc = seq // c
    lall, mall = _gla_constants()
    row = lambda width: pl.BlockSpec((c, width), lambda b, i: (b * nc + i, 0))
    return pl.pallas_call(
        _gla_body,
        grid=(batch, nc),
        in_specs=[row(GLA_QW), row(GLA_QW), row(GLA_QW), row(GLA_VW),
                  pl.BlockSpec((GLA_VW, c), lambda b, i: (0, b * nc + i)), row(GLA_VW),
                  _resident(lall.shape), _resident(mall.shape), _resident((1, GLA_DV))],
        out_specs=row(GLA_VW),
        out_shape=jax.ShapeDtypeStruct((batch * seq, GLA_VW), BF16),
        scratch_shapes=[pltpu.VMEM((GLA_HEADS, GLA_DV, GLA_QW), F32)],
        compiler_params=_cparams(("parallel", "arbitrary")),
        name="gla",
        interpret=interpret,
    )(glq, glk, gla, glv, glvt, glr, lall, mall, gla_out_norm_w[None, :])


def _mem_kv_body(mem_ref, nw_ref, w_ref, kw_ref, bd_ref, mk_ref, mv_ref):
    x = mem_ref[...]
    ms = jnp.mean(x * x, axis=-1, keepdims=True)
    xn = (x * lax.rsqrt(ms + RMS_EPS) * nw_ref[...]).astype(BF16)
    k = _dot(xn, w_ref[:, :XA_W])
    mk_ref[...] = (k * lax.rsqrt(_group_sumsq(k, bd_ref) * (1.0 / XA_HEAD_DIM) + RMS_EPS) * kw_ref[...]).astype(BF16)
    mv_ref[...] = _dot(xn, w_ref[:, XA_W:]).astype(BF16)


def _mem_kv(mem2, mem_norm_w, w_mem_kv, xa_k_norm_w, *, batch, mem_len, interpret):
    blk = pl.BlockSpec((mem_len, XA_W), lambda b: (b, 0))
    return pl.pallas_call(
        _mem_kv_body,
        grid=(batch,),
        in_specs=[pl.BlockSpec((mem_len, D_MODEL), lambda b: (b, 0)), _resident((1, D_MODEL)),
                  _resident((D_MODEL, 2 * XA_W)), _resident((1, XA_W)), _resident((XA_W, XA_W))],
        out_specs=[blk, blk],
        out_shape=[jax.ShapeDtypeStruct((batch * mem_len, XA_W), BF16)] * 2,
        compiler_params=_cparams(("parallel",)),
        name="mem_kv",
        interpret=interpret,
    )(mem2, mem_norm_w[None, :], w_mem_kv.astype(BF16), jnp.tile(xa_k_norm_w, XA_HEADS)[None, :],
      _block_diag_ones(XA_W, XA_HEAD_DIM))


def _piece_offset(j):
    return (j % 2) * (D_MODEL // 2) + (j // 2) * LANES


def _merge_body(x_ref, oda_ref, ogla_ref, xaq_ref, gate_ref, mk_ref, mv_ref, wda_ref, wgla_ref, wxa_ref, wout_ref,
                fw_ref, wq_ref, keys_ref, x1_ref, x8_ref, sc_ref):
    tm = x_ref.shape[0]
    br_xa = None
    for h in range(XA_HEADS):
        hs = slice(h * XA_HEAD_DIM, (h + 1) * XA_HEAD_DIM)
        s = _dot_nt(xaq_ref[:, hs], mk_ref[:, hs])
        s = s - jnp.max(s, axis=-1, keepdims=True)
        p = jnp.exp(s)
        p = p / jnp.sum(p, axis=-1, keepdims=True)
        o = _dot(p.astype(BF16), mv_ref[:, hs])
        t = _dot(o.astype(BF16), wxa_ref[hs, :])
        br_xa = t if br_xa is None else br_xa + t
    merged = (gate_ref[:, 0:D_MODEL].astype(F32) * _dot(oda_ref[...], wda_ref[...])
              + gate_ref[:, D_MODEL:2 * D_MODEL].astype(F32) * _dot(ogla_ref[...], wgla_ref[...])
              + gate_ref[:, 2 * D_MODEL:].astype(F32) * br_xa)
    x1 = x_ref[...] + _dot(merged.astype(BF16), wout_ref[...])
    x1_ref[...] = x1
    ms = jnp.mean(x1 * x1, axis=-1, keepdims=True)
    xn = x1 * lax.rsqrt(ms + RMS_EPS) * fw_ref[...]
    xnb = xn.astype(BF16)
    for j in range(ROW_PIECES):
        off = _piece_offset(j)
        x8_ref[pl.ds(j, tm, stride=ROW_PIECES), :] = xn[:, off:off + LANES]
    pq = _dot(xnb, wq_ref[...])
    half = PEER_QDIM // 2
    for hp in range(2 * PEER_HEADS):
        q_hi, q_lo = _split2(pq[:, hp * half:(hp + 1) * half])
        keys = keys_ref[hp % 2]
        sc_ref[hp] = _dot_nt(keys, q_hi) + _dot_nt(keys, q_lo)


def _merge(x2, oda, ogla, xaq, gates, mk, mv, w_br_da, w_br_gla, w_br_xa, w_out, ffn_norm_w, peer_w_q, peer_sub_keys,
           *, seq, mem_len, tm, interpret):
    n = x2.shape[0]
    spb = seq // tm
    row = lambda width: pl.BlockSpec((tm, width), lambda i: (i, 0))
    mem_blk = pl.BlockSpec((mem_len, XA_W), lambda i: (i // spb, 0))
    nhp = 2 * PEER_HEADS
    return pl.pallas_call(
        _merge_body,
        grid=(n // tm,),
        in_specs=[row(D_MODEL), row(DA_W), row(GLA_VW), row(XA_W), row(GATE_W), mem_blk, mem_blk,
                  _resident((DA_W, D_MODEL)), _resident((GLA_VW, D_MODEL)), _resident((XA_W, D_MODEL)),
                  _resident((D_MODEL, D_MODEL)), _resident((1, D_MODEL)),
                  _resident((D_MODEL, PEER_HEADS * PEER_QDIM)), _resident((2, PEER_N_KEYS, PEER_QDIM // 2))],
        out_specs=[row(D_MODEL), pl.BlockSpec((tm * ROW_PIECES, LANES), lambda i: (i, 0)),
                   pl.BlockSpec((nhp, PEER_N_KEYS, tm), lambda i: (0, 0, i))],
        out_shape=[jax.ShapeDtypeStruct((n, D_MODEL), F32), jax.ShapeDtypeStruct((n * ROW_PIECES, LANES), F32),
                   jax.ShapeDtypeStruct((nhp, PEER_N_KEYS, n), F32)],
        compiler_params=_cparams(("parallel",)),
        name="merge",
        interpret=interpret,
    )(x2, oda, ogla, xaq, gates, mk, mv, w_br_da.astype(BF16), w_br_gla.astype(BF16), w_br_xa.astype(BF16),
      w_out.astype(BF16), ffn_norm_w[None, :], peer_w_q.astype(BF16), peer_sub_keys.astype(BF16))


def _top16(x, code):
    vals, codes = [], []
    for _ in range(PEER_TOPK):
        m = jnp.max(x, axis=0, keepdims=True)
        win = jnp.min(jnp.where(x == m, code, jnp.inf), axis=0, keepdims=True)
        vals.append(m)
        codes.append(win)
        x = jnp.where(code == win, -jnp.inf, x)
    return jnp.concatenate(vals, axis=0), jnp.concatenate(codes, axis=0).astype(jnp.int32)


def _select_rows(table, pos):
    out = jnp.zeros(pos.shape, table.dtype)
    for i in range(PEER_TOPK):
        out = jnp.where(pos == i, table[i:i + 1, :], out)
    return out


def _pair_candidates():
    blocks = [(0, 1, 0, PEER_TOPK)] + [(i, i + 1, 0, SUBLANES) for i in range(1, SUBLANES)]
    blocks.append((SUBLANES, PEER_TOPK, 0, 1))
    codes = np.concatenate([np.array([i * PEER_TOPK + j for i in range(i0, i1) for j in range(j0, j1)])
                            for i0, i1, j0, j1 in blocks])
    return blocks, codes.astype(np.float32)[:, None]


def _peer_topk_body(sc_ref, pair_code_ref, idx_ref, g_ref):
    blocks, _ = _pair_candidates()
    tb = idx_ref.shape[1]
    key_code = lax.broadcasted_iota(jnp.int32, (PEER_N_KEYS, tb), 0).astype(F32)
    pair_code = jnp.broadcast_to(pair_code_ref[...], (pair_code_ref.shape[0], tb))
    experts, gates = [], []
    for h in range(PEER_HEADS):
        s0, i0 = _top16(sc_ref[2 * h], key_code)
        s1, i1 = _top16(sc_ref[2 * h + 1], key_code)
        cand = jnp.concatenate([s0[a0:a1, :] + s1[b0:b1, :] for a0, a1, b0, b1 in blocks], axis=0)
        best, flat = _top16(cand, pair_code)
        expert = (_select_rows(i0, jnp.right_shift(flat, 4)) * PEER_N_KEYS
                  + _select_rows(i1, jnp.bitwise_and(flat, PEER_TOPK - 1)))
        e = jnp.exp(best - best[0:1, :])
        odd = jnp.bitwise_and(lax.broadcasted_iota(jnp.int32, expert.shape, 0), 1)
        experts.append(expert * ROW_SUBLANES + (SUBLANES - ROW_SUBLANES * odd))
        gates.append(e / jnp.sum(e, axis=0, keepdims=True))
    idx_ref[...] = jnp.concatenate(experts, axis=0)
    g_ref[...] = jnp.concatenate(gates, axis=0).T


def _peer_topk(scores, *, tb, interpret):
    nhp, nk, n = scores.shape
    pair_code = jnp.asarray(_pair_candidates()[1])
    return pl.pallas_call(
        _peer_topk_body,
        grid=(n // tb,),
        in_specs=[pl.BlockSpec((nhp, nk, tb), lambda i: (0, 0, i)), _resident(pair_code.shape)],
        out_specs=[pl.BlockSpec((PEER_SLOTS, tb), lambda i: (0, i)), pl.BlockSpec((tb, PEER_SLOTS), lambda i: (i, 0))],
        out_shape=[jax.ShapeDtypeStruct((PEER_SLOTS, n), jnp.int32), jax.ShapeDtypeStruct((n, PEER_SLOTS), F32)],
        compiler_params=_cparams(("parallel",)),
        name="peer_topk",
        interpret=interpret,
    )(scores, pair_code)


def _pack_table(t):
    e = t.shape[0]
    bits = lax.bitcast_convert_type(t.astype(BF16), jnp.uint16).astype(jnp.uint32)
    bits = bits.reshape(e, 2, ROW_SUBLANES, LANES)
    packed = (bits[:, 0] | (bits[:, 1] << 16)).reshape(e * ROW_SUBLANES, LANES)
    pad = jnp.zeros((SUBLANES, LANES), jnp.uint32)
    return jnp.concatenate([pad, packed, pad], axis=0)


def _expert_rows(idx_scr, tab_ref, t):
    low = lax.broadcasted_iota(jnp.int32, (SUBLANES, LANES), 0) < ROW_SUBLANES
    tiles = [tab_ref[pl.ds(pl.multiple_of(idx_scr.at[k][t], ROW_SUBLANES), SUBLANES), :] for k in range(PEER_SLOTS)]
    pairs = [jnp.where(low, tiles[2 * m], tiles[2 * m + 1]) for m in range(PEER_SLOTS // 2)]
    return pltpu.bitcast(jnp.concatenate(pairs, axis=0), BF16)


def _token_blocks(idx_hbm, idx_a, idx_b, sem, per_token, tb):
    i = pl.program_id(0)

    def copy(blk, scr, s):
        return pltpu.make_async_copy(idx_hbm.at[:, pl.ds(pl.multiple_of(blk * tb, LANES), tb)], scr, sem.at[s])

    def run_block(idx_scr, base):
        def body(j, carry):
            for u in range(PEER_TOKENS_PER_STEP):
                per_token(idx_scr, PEER_TOKENS_PER_STEP * j + u, base)
            return carry

        lax.fori_loop(0, tb // PEER_TOKENS_PER_STEP, body, 0)

    @pl.when(i == 0)
    def _():
        copy(0, idx_a, 0).start()

    copy(2 * i, idx_a, 0).wait()
    copy(2 * i + 1, idx_b, 1).start()
    run_block(idx_a, 0)
    copy(2 * i + 1, idx_b, 1).wait()

    @pl.when(i + 1 < pl.num_programs(0))
    def _():
        copy(2 * i + 2, idx_a, 0).start()

    run_block(idx_b, tb)


def _piece_mask():
    r = lax.broadcasted_iota(jnp.int32, (ROW_PIECES, PEER_SLOTS * ROW_PIECES), 0)
    c = lax.broadcasted_iota(jnp.int32, (ROW_PIECES, PEER_SLOTS * ROW_PIECES), 1)
    return (c % ROW_PIECES == r).astype(F32)


def _peer_u_body(idx_hbm, x8_ref, g_ref, tab_ref, sel_ref, selt_ref, w_ref, idx_a, idx_b, sem, r_scr, *, tb):
    dmask = _piece_mask()

    def per_token(idx_scr, t, base):
        x8 = x8_ref[pl.ds(pl.multiple_of((base + t) * ROW_PIECES, ROW_PIECES), ROW_PIECES), :].astype(BF16)
        r = _dot_nt(x8, _expert_rows(idx_scr, tab_ref, t))
        r_scr[pl.ds(base + t, 1), :] = jnp.sum(r * dmask, axis=0, keepdims=True)

    _token_blocks(idx_hbm, idx_a, idx_b, sem, per_token, tb)
    hi, lo = _split2(r_scr[...])
    s = _dot(hi, sel_ref[...]) + _dot(lo, sel_ref[...])
    act = 0.5 * s * (1.0 + lax.erf(s * (2.0 ** -0.5)))
    w = (g_ref[...] * act).astype(BF16)
    w_ref[...] = _dot(w, selt_ref[...])


def _peer_v_body(idx_hbm, w_ref, x1_ref, tab_ref, o_ref, idx_a, idx_b, sem, o8_scr, *, tb):
    dmask = _piece_mask()

    def per_token(idx_scr, t, base):
        lhs = (w_ref[pl.ds(base + t, 1), :] * dmask).astype(BF16)
        o8_scr[pl.ds(pl.multiple_of((base + t) * ROW_PIECES, ROW_PIECES), ROW_PIECES), :] = _dot(
            lhs, _expert_rows(idx_scr, tab_ref, t))

    _token_blocks(idx_hbm, idx_a, idx_b, sem, per_token, tb)
    for j in range(ROW_PIECES):
        cs = slice(_piece_offset(j), _piece_offset(j) + LANES)
        o_ref[:, cs] = x1_ref[:, cs] + o8_scr[pl.ds(j, 2 * tb, stride=ROW_PIECES), :]


def _idx_scratch(tb):
    return [pltpu.SMEM((PEER_SLOTS, tb), jnp.int32), pltpu.SMEM((PEER_SLOTS, tb), jnp.int32),
            pltpu.SemaphoreType.DMA((2,))]


def _peer_u(idx_t, x8, gates, table, *, tb, interpret):
    n = idx_t.shape[1]
    ts = 2 * tb
    sel_np = (np.arange(PEER_SLOTS * ROW_PIECES)[:, None] // ROW_PIECES == np.arange(PEER_SLOTS)[None, :])
    sel = jnp.asarray(sel_np.astype(np.float32), dtype=BF16)
    return pl.pallas_call(
        functools.partial(_peer_u_body, tb=tb),
        grid=(n // ts,),
        in_specs=[pl.BlockSpec(memory_space=pl.ANY),
                  pl.BlockSpec((ts * ROW_PIECES, LANES), lambda i: (i, 0)),
                  pl.BlockSpec((ts, PEER_SLOTS), lambda i: (i, 0)),
                  _resident(table.shape), _resident(sel.shape), _resident(sel.T.shape)],
        out_specs=pl.BlockSpec((ts, PEER_SLOTS * ROW_PIECES), lambda i: (i, 0)),
        out_shape=jax.ShapeDtypeStruct((n, PEER_SLOTS * ROW_PIECES), F32),
        scratch_shapes=_idx_scratch(tb) + [pltpu.VMEM((ts, PEER_SLOTS * ROW_PIECES), F32)],
        compiler_params=_cparams(("arbitrary",)),
        name="peer_u",
        interpret=interpret,
    )(idx_t, x8, gates, table, sel, sel.T)


def _peer_v(idx_t, w_exp, x1, table, *, tb, interpret):
    n = idx_t.shape[1]
    ts = 2 * tb
    return pl.pallas_call(
        functools.partial(_peer_v_body, tb=tb),
        grid=(n // ts,),
        in_specs=[pl.BlockSpec(memory_space=pl.ANY),
                  pl.BlockSpec((ts, PEER_SLOTS * ROW_PIECES), lambda i: (i, 0)),
                  pl.BlockSpec((ts, D_MODEL), lambda i: (i, 0)),
                  _resident(table.shape)],
        out_specs=pl.BlockSpec((ts, D_MODEL), lambda i: (i, 0)),
        out_shape=jax.ShapeDtypeStruct((n, D_MODEL), F32),
        scratch_shapes=_idx_scratch(tb) + [pltpu.VMEM((ts * ROW_PIECES, LANES), F32)],
        compiler_params=_cparams(("arbitrary",)),
        name="peer_v",
        interpret=interpret,
    )(idx_t, w_exp, x1, table)


def _layer(x2, mem2, p, *, layer, batch, seq, mem_len, tm, attn_blk_q, attn_blk_k, peer_tb, interpret):
    lam_init = 0.8 - 0.6 * math.exp(-0.3 * layer)
    (daq, dak, dav, glq, glk, glv, glvt, glr, gla, xaq, gates) = _in_proj(
        x2, p["mix_norm_w"], p["w_in"], p["da_q_norm_w"], p["da_k_norm_w"], p["xa_q_norm_w"], p["gla_w_gate"],
        p["gla_b_gate"], seq=seq, tm=tm, interpret=interpret)
    oda = _diff_attn(daq, dak, dav, p["da_lambda"], p["da_out_norm_w"], batch=batch, seq=seq, blk_q=attn_blk_q,
                     blk_k=attn_blk_k, lam_init=lam_init, interpret=interpret)
    ogla = _gla(glq, glk, gla, glv, glvt, glr, p["gla_out_norm_w"], batch=batch, seq=seq, interpret=interpret)
    mk, mv = _mem_kv(mem2, p["mem_norm_w"], p["w_mem_kv"], p["xa_k_norm_w"], batch=batch, mem_len=mem_len,
                     interpret=interpret)
    x1, x8, scores = _merge(x2, oda, ogla, xaq, gates, mk, mv, p["w_br_da"], p["w_br_gla"], p["w_br_xa"], p["w_out"],
                             p["ffn_norm_w"], p["peer_w_q"], p["peer_sub_keys"], seq=seq, mem_len=mem_len, tm=tm,
                             interpret=interpret)
    idx_t, peer_gates = _peer_topk(scores, tb=peer_tb, interpret=interpret)
    w_exp = _peer_u(idx_t, x8, peer_gates, _pack_table(p["peer_u"]), tb=peer_tb, interpret=interpret)
    return _peer_v(idx_t, w_exp, x1, _pack_table(p["peer_v"]), tb=peer_tb, interpret=interpret)


_PARAM_NAMES = ("mix_norm_w", "w_in", "da_q_norm_w", "da_k_norm_w", "da_lambda", "da_out_norm_w", "gla_w_gate",
                "gla_b_gate", "gla_out_norm_w", "mem_norm_w", "w_mem_kv", "xa_q_norm_w", "xa_k_norm_w", "w_br_da",
                "w_br_gla", "w_br_xa", "w_out", "ffn_norm_w", "peer_w_q", "peer_sub_keys", "peer_u", "peer_v")


def _forward(x, mem, params, *, tm=256, attn_blk_q=512, attn_blk_k=2048, peer_tb=128, interpret=False):
    batch, seq, d = x.shape
    mem_len = mem.shape[1]
    x2 = x.reshape(batch * seq, d)
    mem2 = mem.reshape(batch * mem_len, d)
    depth = params["w_in"].shape[0]
    for layer in range(depth):
        p = {name: params[name][layer] for name in _PARAM_NAMES}
        x2 = _layer(x2, mem2, p, layer=layer, batch=batch, seq=seq, mem_len=mem_len, tm=tm, attn_blk_q=attn_blk_q,
                    attn_blk_k=attn_blk_k, peer_tb=peer_tb, interpret=interpret)
    return x2.reshape(batch, seq, d)


def kernel(x, mem, mix_norm_w, w_in, da_q_norm_w, da_k_norm_w, da_lambda, da_out_norm_w, gla_w_gate, gla_b_gate,
           gla_out_norm_w, mem_norm_w, w_mem_kv, xa_q_norm_w, xa_k_norm_w, w_br_da, w_br_gla, w_br_xa, w_out,
           ffn_norm_w, peer_w_q, peer_sub_keys, peer_u, peer_v):
    params = dict(zip(_PARAM_NAMES, (mix_norm_w, w_in, da_q_norm_w, da_k_norm_w, da_lambda, da_out_norm_w, gla_w_gate,
                                     gla_b_gate, gla_out_norm_w, mem_norm_w, w_mem_kv, xa_q_norm_w, xa_k_norm_w,
                                     w_br_da, w_br_gla, w_br_xa, w_out, ffn_norm_w, peer_w_q, peer_sub_keys, peer_u,
                                     peer_v)))
    return _forward(x, mem, params)
```

```python
import functools
import math

import jax
import jax.numpy as jnp
import numpy as np
from jax import lax
from jax.experimental import pallas as pl
from jax.experimental.pallas import tpu as pltpu

F32 = jnp.float32
BF16 = jnp.bfloat16

LANES = 128
SUBLANES = 8
VMEM_LIMIT_BYTES = 56 * 1024 * 1024

D_MODEL = 1024
DA_HEADS = 4
DA_HEAD_DIM = 64
DA_V_DIM = 2 * DA_HEAD_DIM
GLA_HEADS = 4
GLA_DK = 64
GLA_DV = 128
GLA_GATE_RANK = 16
GLA_TAU = 16.0
XA_HEADS = 4
XA_HEAD_DIM = 128
N_BRANCH = 3
PEER_HEADS = 8
PEER_N_KEYS = 128
PEER_QDIM = 256
PEER_TOPK = 16
RMS_EPS = 1e-6
LOG2E = math.log2(math.e)

DA_W = DA_HEADS * 2 * DA_HEAD_DIM
GLA_QW = GLA_HEADS * GLA_DK
GLA_VW = GLA_HEADS * GLA_DV
XA_W = XA_HEADS * XA_HEAD_DIM
GATE_W = N_BRANCH * D_MODEL
PEER_SLOTS = PEER_HEADS * PEER_TOPK

C_DAQ = 0
C_DAK = C_DAQ + DA_W
C_DAV = C_DAK + DA_W
C_GLQ = C_DAV + DA_W
C_GLK = C_GLQ + GLA_QW
C_GLV = C_GLK + GLA_QW
C_GLR = C_GLV + GLA_VW
C_XAQ = C_GLR + GLA_VW
C_GATE = C_XAQ + XA_W
C_GLG = C_GATE + GATE_W
C_END = C_GLG + LANES

GLA_CHUNK = 128
GLA_LEVELS = 7

ROW_SUBLANES = 4
ROW_PIECES = 2 * ROW_SUBLANES
PEER_TOKENS_PER_STEP = 16


def _cparams(sem):
    return pltpu.CompilerParams(dimension_semantics=sem, vmem_limit_bytes=VMEM_LIMIT_BYTES)


def _resident(shape):
    nd = len(shape)
    return pl.BlockSpec(shape, lambda *_: (0,) * nd, pipeline_mode=pl.Buffered(1))


def _dot(a, b):
    return jnp.dot(a, b, preferred_element_type=F32)


def _dot_nt(a, b):
    return lax.dot_general(a, b, (((1,), (1,)), ((), ())), preferred_element_type=F32)


def _split2(t):
    hi = t.astype(BF16)
    lo = (t - hi.astype(F32)).astype(BF16)
    return hi, lo


def _group_sumsq(t, bd_ref):
    hi, lo = _split2(t * t)
    bd = bd_ref[...]
    return _dot(hi, bd) + _dot(lo, bd)


def _in_proj_body(x_ref, nw_ref, w_ref, wvt_ref, qw_ref, kw_ref, xw_ref, wg_ref, bg_ref, bd64_ref, bd128_ref,
                  daq_ref, dak_ref, dav_ref, glq_ref, glk_ref, glv_ref, glvt_ref, glr_ref, gla_ref, xaq_ref,
                  gate_ref, *, seq):
    x = x_ref[...]
    ms = jnp.mean(x * x, axis=-1, keepdims=True)
    xn = (x * lax.rsqrt(ms + RMS_EPS) * nw_ref[...]).astype(BF16)

    def proj(c0, width):
        return _dot(xn, w_ref[:, c0:c0 + width])

    q = proj(C_DAQ, DA_W)
    q = q * lax.rsqrt(_group_sumsq(q, bd64_ref) * (1.0 / DA_HEAD_DIM) + RMS_EPS) * qw_ref[...]
    k = proj(C_DAK, DA_W)
    k = k * lax.rsqrt(_group_sumsq(k, bd64_ref) * (1.0 / DA_HEAD_DIM) + RMS_EPS) * kw_ref[...]
    tm = x.shape[0]
    lane = lax.broadcasted_iota(jnp.int32, (1, LANES), 1)
    pos = (pl.program_id(0) * tm + lax.broadcasted_iota(jnp.int32, (tm, 1), 0)) % seq
    pos = pos.astype(F32)
    for h in range(DA_HEADS):
        hs = slice(h * DA_V_DIM, (h + 1) * DA_V_DIM)
        b0 = pos * (LOG2E * 2.0 ** (-8.0 * (h + 1) / DA_HEADS))
        p0 = b0.astype(BF16).astype(F32)
        b1 = b0 - p0
        p1 = b1.astype(BF16).astype(F32)
        p2 = b1 - p1
        for c in range(2):
            own = (lane < DA_HEAD_DIM) if c == 0 else (lane >= DA_HEAD_DIM)
            first = DA_HEAD_DIM * (1 - c)
            cs = slice((2 * h + c) * LANES, (2 * h + c + 1) * LANES)
            ones = ((lane >= first) & (lane < first + 3)).astype(F32)
            daq_ref[:, cs] = jnp.where(own, q[:, hs], ones).astype(BF16)
            feat = jnp.where(lane == first, p0, jnp.where(lane == first + 1, p1, jnp.where(lane == first + 2, p2, 0.0)))
            dak_ref[:, cs] = jnp.where(own, k[:, hs], feat).astype(BF16)
    dav_ref[...] = proj(C_DAV, DA_W).astype(BF16)

    glq_ref[...] = proj(C_GLQ, GLA_QW) * (GLA_DK ** -0.5)
    glk_ref[...] = proj(C_GLK, GLA_QW)
    glv_ref[...] = proj(C_GLV, GLA_VW).astype(BF16)
    glvt_ref[...] = _dot_nt(wvt_ref[...], xn).astype(BF16)
    r = proj(C_GLR, GLA_VW)
    glr_ref[...] = (r * jax.nn.sigmoid(r)).astype(BF16)

    g = proj(C_GLG, LANES)
    z = jnp.dot(g, wg_ref[...], preferred_element_type=F32, precision=lax.Precision.HIGHEST) + bg_ref[...]
    log_sig = jnp.minimum(z, 0.0) - jnp.log1p(jnp.exp(-jnp.abs(z)))
    gla_ref[...] = log_sig * (1.0 / GLA_TAU)

    xq = proj(C_XAQ, XA_W)
    xaq_ref[...] = (xq * lax.rsqrt(_group_sumsq(xq, bd128_ref) * (1.0 / XA_HEAD_DIM) + RMS_EPS) * xw_ref[...]).astype(BF16)

    for c in range(0, GATE_W, 512):
        gate_ref[:, c:c + 512] = jax.nn.sigmoid(proj(C_GATE + c, 512)).astype(BF16)


def _block_diag_ones(width, group):
    idx = np.arange(width) // group
    return jnp.asarray((idx[:, None] == idx[None, :]).astype(np.float32), dtype=BF16)


def _in_proj(x2, mix_norm_w, w_in, da_q_norm_w, da_k_norm_w, xa_q_norm_w, gla_w_gate, gla_b_gate, *, seq, tm,
             interpret):
    n = x2.shape[0]
    c_glg_src = C_GLR + GLA_VW
    w = jnp.concatenate(
        [w_in[:, :c_glg_src], w_in[:, c_glg_src + GLA_GATE_RANK:], w_in[:, c_glg_src:c_glg_src + GLA_GATE_RANK],
         jnp.zeros((D_MODEL, LANES - GLA_GATE_RANK), w_in.dtype)], axis=1).astype(BF16)
    wvt = w_in[:, C_GLV:C_GLV + GLA_VW].T.astype(BF16)
    qw = jnp.tile(da_q_norm_w, 2 * DA_HEADS)[None, :] * (DA_HEAD_DIM ** -0.5 * LOG2E)
    kw = jnp.tile(da_k_norm_w, 2 * DA_HEADS)[None, :]
    xw = jnp.tile(xa_q_norm_w, XA_HEADS)[None, :] * (XA_HEAD_DIM ** -0.5)
    wg = jnp.zeros((LANES, GLA_QW), F32).at[:GLA_GATE_RANK].set(gla_w_gate)
    bg = gla_b_gate[None, :]
    row = lambda width: pl.BlockSpec((tm, width), lambda i: (i, 0))
    out_widths = [(2 * DA_W, BF16), (2 * DA_W, BF16), (DA_W, BF16), (GLA_QW, F32), (GLA_QW, F32), (GLA_VW, BF16)]
    out_shape = [jax.ShapeDtypeStruct((n, wd), dt) for wd, dt in out_widths]
    out_specs = [row(wd) for wd, _ in out_widths]
    out_shape.append(jax.ShapeDtypeStruct((GLA_VW, n), BF16))
    out_specs.append(pl.BlockSpec((GLA_VW, tm), lambda i: (0, i)))
    for wd, dt in [(GLA_VW, BF16), (GLA_QW, F32), (XA_W, BF16), (GATE_W, BF16)]:
        out_shape.append(jax.ShapeDtypeStruct((n, wd), dt))
        out_specs.append(row(wd))
    return pl.pallas_call(
        functools.partial(_in_proj_body, seq=seq),
        grid=(n // tm,),
        in_specs=[row(D_MODEL), _resident((1, D_MODEL)), _resident((D_MODEL, C_END)), _resident((GLA_VW, D_MODEL)),
                  _resident((1, DA_W)), _resident((1, DA_W)), _resident((1, XA_W)), _resident((LANES, GLA_QW)),
                  _resident((1, GLA_QW)), _resident((DA_W, DA_W)), _resident((XA_W, XA_W))],
        out_specs=out_specs,
        out_shape=out_shape,
        compiler_params=_cparams(("parallel",)),
        name="in_proj",
        interpret=interpret,
    )(x2, mix_norm_w[None, :], w, wvt, qw, kw, xw, wg, bg, _block_diag_ones(DA_W, DA_HEAD_DIM),
      _block_diag_ones(XA_W, XA_HEAD_DIM))


def _diff_attn_body(qi_tab, kj_tab, q_ref, k_ref, v_ref, lam_ref, ow_ref, o_ref, m_scr, acc_scr, *, blk_q, blk_k,
                    lam_init):
    p = pl.program_id(1)
    qi = qi_tab[p]
    kj = kj_tab[p]
    last = (qi * blk_q) // blk_k

    @pl.when(kj == 0)
    def _():
        m_scr[...] = jnp.full(m_scr.shape, -jnp.inf, F32)
        acc_scr[...] = jnp.zeros(acc_scr.shape, F32)

    def component(h, c, causal):
        v = v_ref[:, h * DA_V_DIM:(h + 1) * DA_V_DIM]
        v_ones = jnp.concatenate([v, jnp.ones_like(v)], axis=1)
        cs = slice((2 * h + c) * LANES, (2 * h + c + 1) * LANES)
        s = _dot_nt(q_ref[:, cs], k_ref[:, cs])
        if causal:
            row = lax.broadcasted_iota(jnp.int32, (blk_q, 1), 0) + qi * blk_q
            col = lax.broadcasted_iota(jnp.int32, (1, blk_k), 1) + kj * blk_k
            s = jnp.where(col <= row, s, -jnp.inf)
        i = 2 * h + c
        m_prev = m_scr[i]
        m_new = jnp.maximum(m_prev, jnp.max(s, axis=-1, keepdims=True))
        acc_scr[i] = jnp.exp2(m_prev - m_new) * acc_scr[i] + _dot(jnp.exp2(s - m_new).astype(BF16), v_ones)
        m_scr[i] = m_new

    @pl.when(kj < last)
    def _():
        for h in range(DA_HEADS):
            component(h, 0, False)
            component(h, 1, False)

    @pl.when(kj == last)
    def _():
        lv = lam_ref[...]
        lam = (jnp.exp(jnp.sum(lv[0:1] * lv[1:2], axis=-1, keepdims=True))
               - jnp.exp(jnp.sum(lv[2:3] * lv[3:4], axis=-1, keepdims=True)) + lam_init)
        for h in range(DA_HEADS):
            component(h, 0, True)
            component(h, 1, True)
            a0 = acc_scr[2 * h]
            a1 = acc_scr[2 * h + 1]
            o = a0[:, :DA_V_DIM] / a0[:, DA_V_DIM:] - lam * (a1[:, :DA_V_DIM] / a1[:, DA_V_DIM:])
            ms = jnp.mean(o * o, axis=-1, keepdims=True)
            o_ref[:, h * DA_V_DIM:(h + 1) * DA_V_DIM] = (
                o * lax.rsqrt(ms + RMS_EPS) * ow_ref[...] * (1.0 - lam_init)).astype(BF16)


def _diff_attn(daq, dak, dav, da_lambda, da_out_norm_w, *, batch, seq, blk_q, blk_k, lam_init, interpret):
    assert blk_k % blk_q == 0 and seq % blk_k == 0
    nq = seq // blk_q
    nk = seq // blk_k
    pairs = [(qi, kj) for qi in range(nq) for kj in range((qi * blk_q) // blk_k + 1)]
    qi_tab = jnp.asarray([p[0] for p in pairs], jnp.int32)
    kj_tab = jnp.asarray([p[1] for p in pairs], jnp.int32)
    qmap = lambda b, p, qt, kt: (b * nq + qt[p], 0)
    kmap = lambda b, p, qt, kt: (b * nk + kt[p], 0)
    n_comp = 2 * DA_HEADS
    grid_spec = pltpu.PrefetchScalarGridSpec(
        num_scalar_prefetch=2,
        grid=(batch, len(pairs)),
        in_specs=[pl.BlockSpec((blk_q, n_comp * LANES), qmap), pl.BlockSpec((blk_k, n_comp * LANES), kmap),
                  pl.BlockSpec((blk_k, DA_HEADS * DA_V_DIM), kmap),
                  pl.BlockSpec((4, DA_HEAD_DIM), lambda b, p, qt, kt: (0, 0)),
                  pl.BlockSpec((1, DA_V_DIM), lambda b, p, qt, kt: (0, 0))],
        out_specs=pl.BlockSpec((blk_q, DA_HEADS * DA_V_DIM), qmap),
        scratch_shapes=[pltpu.VMEM((n_comp, blk_q, 1), F32), pltpu.VMEM((n_comp, blk_q, 2 * DA_V_DIM), F32)],
    )
    return pl.pallas_call(
        functools.partial(_diff_attn_body, blk_q=blk_q, blk_k=blk_k, lam_init=lam_init),
        grid_spec=grid_spec,
        out_shape=jax.ShapeDtypeStruct((batch * seq, DA_HEADS * DA_V_DIM), BF16),
        compiler_params=_cparams(("parallel", "arbitrary")),
        name="diff_attn",
        interpret=interpret,
    )(qi_tab, kj_tab, daq, dak, dav, da_lambda, da_out_norm_w[None, :])


def _gla_constants():
    c = GLA_CHUNK
    t = np.arange(c)[:, None]
    u = np.arange(c)[None, :]
    lmats, masks = [], []
    for lev in range(GLA_LEVELS):
        m = 1 << lev
        second = (t % (2 * m)) >= m
        first = ~second
        bnd = (t // m) * m
        lmats.append(second & (u >= bnd) & (u <= t))
    for lev in range(GLA_LEVELS):
        m = 1 << lev
        first = (t % (2 * m)) < m
        end = (t // m) * m + m - 1
        lmats.append(first & (u > t) & (u <= end))
    lmats.append(u <= t)
    lmats.append(u > t)
    for lev in range(GLA_LEVELS):
        m = 1 << lev
        s = np.arange(c)[None, :]
        masks.append(((t // (2 * m)) == (s // (2 * m))) & ((t % (2 * m)) >= m) & ((s % (2 * m)) < m))
    masks.append(t == np.arange(c)[None, :])
    lall = jnp.asarray(np.concatenate(lmats, axis=0).astype(np.float32), dtype=BF16)
    mall = jnp.asarray(np.stack(masks, axis=0).astype(np.float32))
    return lall, mall


def _gla_body(q_ref, k_ref, la_ref, v_ref, vt_ref, r_ref, lall_ref, mall_ref, ow_ref, o_ref, state_scr):
    c = GLA_CHUNK

    @pl.when(pl.program_id(1) == 0)
    def _():
        state_scr[...] = jnp.zeros(state_scr.shape, F32)

    q = q_ref[...]
    k = k_ref[...]
    g = la_ref[...]
    g1 = g.astype(BF16)
    r1 = g - g1.astype(F32)
    g2 = r1.astype(BF16)
    g3 = (r1 - g2.astype(F32)).astype(BF16)
    lall = lall_ref[...]
    e_all = jnp.exp(_dot(lall, g1) + _dot(lall, g2) + _dot(lall, g3))

    def rows(i):
        return e_all[i * c:(i + 1) * c]

    lane = lax.broadcasted_iota(jnp.int32, (1, GLA_QW), 1)
    head_masks = [(lane // GLA_DK == h).astype(F32) for h in range(GLA_HEADS)]
    a = [jnp.zeros((c, c), F32) for _ in range(GLA_HEADS)]
    for lev in range(GLA_LEVELS + 1):
        if lev < GLA_LEVELS:
            ql = q * rows(lev)
            kl = (k * rows(GLA_LEVELS + lev)).astype(BF16)
        else:
            ql = q
            kl = k.astype(BF16)
        mask = mall_ref[lev]
        for h in range(GLA_HEADS):
            a[h] = a[h] + mask * _dot_nt((ql * head_masks[h]).astype(BF16), kl)

    e_b = rows(2 * GLA_LEVELS)
    q_dec = q * e_b
    k_dec = (k * rows(2 * GLA_LEVELS + 1)).astype(BF16)
    chunk_decay = e_b[c - 1:c, :]
    for h in range(GLA_HEADS):
        vs = slice(h * GLA_DV, (h + 1) * GLA_DV)
        st = state_scr[h]
        o = _dot(a[h].astype(BF16), v_ref[:, vs]) + _dot_nt((q_dec * head_masks[h]).astype(BF16), st.astype(BF16))
        state_scr[h] = st * chunk_decay + _dot(vt_ref[vs, :], k_dec)
        ms = jnp.mean(o * o, axis=-1, keepdims=True)
        o_ref[:, vs] = (o * lax.rsqrt(ms + RMS_EPS) * ow_ref[...] * r_ref[:, vs].astype(F32)).astype(BF16)


def _gla(glq, glk, gla, glv, glvt, glr, gla_out_norm_w, *, batch, seq, interpret):
    c = GLA_CHUNK
    nc = seq // c
    lall, mall = _gla_constants()
    row = lambda width: pl.BlockSpec((c, width), lambda b, i: (b * nc + i, 0))
    return pl.pallas_call(
        _gla_body,
        grid=(batch, nc),
        in_specs=[row(GLA_QW), row(GLA_QW), row(GLA_QW), row(GLA_VW),
                  pl.BlockSpec((GLA_VW, c), lambda b, i: (0, b * nc + i)), row(GLA_VW),
                  _resident(lall.shape), _resident(mall.shape), _resident((1, GLA_DV))],
        out_specs=row(GLA_VW),
        out_shape=jax.ShapeDtypeStruct((batch * seq, GLA_VW), BF16),
        scratch_shapes=[pltpu.VMEM((GLA_HEADS, GLA_DV, GLA_QW), F32)],
        compiler_params=_cparams(("parallel", "arbitrary")),
        name="gla",
        interpret=interpret,
    )(glq, glk, gla, glv, glvt, glr, lall, mall, gla_out_norm_w[None, :])


def _mem_kv_body(mem_ref, nw_ref, w_ref, kw_ref, bd_ref, mk_ref, mv_ref):
    x = mem_ref[...]
    ms = jnp.mean(x * x, axis=-1, keepdims=True)
    xn = (x * lax.rsqrt(ms + RMS_EPS) * nw_ref[...]).astype(BF16)
    k = _dot(xn, w_ref[:, :XA_W])
    mk_ref[...] = (k * lax.rsqrt(_group_sumsq(k, bd_ref) * (1.0 / XA_HEAD_DIM) + RMS_EPS) * kw_ref[...]).astype(BF16)
    mv_ref[...] = _dot(xn, w_ref[:, XA_W:]).astype(BF16)


def _mem_kv(mem2, mem_norm_w, w_mem_kv, xa_k_norm_w, *, batch, mem_len, interpret):
    blk = pl.BlockSpec((mem_len, XA_W), lambda b: (b, 0))
    return pl.pallas_call(
        _mem_kv_body,
        grid=(batch,),
        in_specs=[pl.BlockSpec((mem_len, D_MODEL), lambda b: (b, 0)), _resident((1, D_MODEL)),
                  _resident((D_MODEL, 2 * XA_W)), _resident((1, XA_W)), _resident((XA_W, XA_W))],
        out_specs=[blk, blk],
        out_shape=[jax.ShapeDtypeStruct((batch * mem_len, XA_W), BF16)] * 2,
        compiler_params=_cparams(("parallel",)),
        name="mem_kv",
        interpret=interpret,
    )(mem2, mem_norm_w[None, :], w_mem_kv.astype(BF16), jnp.tile(xa_k_norm_w, XA_HEADS)[None, :],
      _block_diag_ones(XA_W, XA_HEAD_DIM))


def _piece_offset(j):
    return (j % 2) * (D_MODEL // 2) + (j // 2) * LANES


def _merge_body(x_ref, oda_ref, ogla_ref, xaq_ref, gate_ref, mk_ref, mv_ref, wda_ref, wgla_ref, wxa_ref, wout_ref,
                fw_ref, wq_ref, keys_ref, x1_ref, x8_ref, sc_ref):
    tm = x_ref.shape[0]
    br_xa = None
    for h in range(XA_HEADS):
        hs = slice(h * XA_HEAD_DIM, (h + 1) * XA_HEAD_DIM)
        s = _dot_nt(xaq_ref[:, hs], mk_ref[:, hs])
        s = s - jnp.max(s, axis=-1, keepdims=True)
        p = jnp.exp(s)
        p = p / jnp.sum(p, axis=-1, keepdims=True)
        o = _dot(p.astype(BF16), mv_ref[:, hs])
        t = _dot(o.astype(BF16), wxa_ref[hs, :])
        br_xa = t if br_xa is None else br_xa + t
    merged = (gate_ref[:, 0:D_MODEL].astype(F32) * _dot(oda_ref[...], wda_ref[...])
              + gate_ref[:, D_MODEL:2 * D_MODEL].astype(F32) * _dot(ogla_ref[...], wgla_ref[...])
              + gate_ref[:, 2 * D_MODEL:].astype(F32) * br_xa)
    x1 = x_ref[...] + _dot(merged.astype(BF16), wout_ref[...])
    x1_ref[...] = x1
    ms = jnp.mean(x1 * x1, axis=-1, keepdims=True)
    xn = x1 * lax.rsqrt(ms + RMS_EPS) * fw_ref[...]
    xnb = xn.astype(BF16)
    for j in range(ROW_PIECES):
        off = _piece_offset(j)
        x8_ref[pl.ds(j, tm, stride=ROW_PIECES), :] = xn[:, off:off + LANES]
    pq = _dot(xnb, wq_ref[...])
    half = PEER_QDIM // 2
    for hp in range(2 * PEER_HEADS):
        q_hi, q_lo = _split2(pq[:, hp * half:(hp + 1) * half])
        keys = keys_ref[hp % 2]
        sc_ref[hp] = _dot_nt(keys, q_hi) + _dot_nt(keys, q_lo)


def _merge(x2, oda, ogla, xaq, gates, mk, mv, w_br_da, w_br_gla, w_br_xa, w_out, ffn_norm_w, peer_w_q, peer_sub_keys,
           *, seq, mem_len, tm, interpret):
    n = x2.shape[0]
    spb = seq // tm
    row = lambda width: pl.BlockSpec((tm, width), lambda i: (i, 0))
    mem_blk = pl.BlockSpec((mem_len, XA_W), lambda i: (i // spb, 0))
    nhp = 2 * PEER_HEADS
    return pl.pallas_call(
        _merge_body,
        grid=(n // tm,),
        in_specs=[row(D_MODEL), row(DA_W), row(GLA_VW), row(XA_W), row(GATE_W), mem_blk, mem_blk,
                  _resident((DA_W, D_MODEL)), _resident((GLA_VW, D_MODEL)), _resident((XA_W, D_MODEL)),
                  _resident((D_MODEL, D_MODEL)), _resident((1, D_MODEL)),
                  _resident((D_MODEL, PEER_HEADS * PEER_QDIM)), _resident((2, PEER_N_KEYS, PEER_QDIM // 2))],
        out_specs=[row(D_MODEL), pl.BlockSpec((tm * ROW_PIECES, LANES), lambda i: (i, 0)),
                   pl.BlockSpec((nhp, PEER_N_KEYS, tm), lambda i: (0, 0, i))],
        out_shape=[jax.ShapeDtypeStruct((n, D_MODEL), F32), jax.ShapeDtypeStruct((n * ROW_PIECES, LANES), F32),
                   jax.ShapeDtypeStruct((nhp, PEER_N_KEYS, n), F32)],
        compiler_params=_cparams(("parallel",)),
        name="merge",
        interpret=interpret,
    )(x2, oda, ogla, xaq, gates, mk, mv, w_br_da.astype(BF16), w_br_gla.astype(BF16), w_br_xa.astype(BF16),
      w_out.astype(BF16), ffn_norm_w[None, :], peer_w_q.astype(BF16), peer_sub_keys.astype(BF16))


def _top16(x, code):
    vals, codes = [], []
    for _ in range(PEER_TOPK):
        m = jnp.max(x, axis=0, keepdims=True)
        win = jnp.min(jnp.where(x == m, code, jnp.inf), axis=0, keepdims=True)
        vals.append(m)
        codes.append(win)
        x = jnp.where(code == win, -jnp.inf, x)
    return jnp.concatenate(vals, axis=0), jnp.concatenate(codes, axis=0).astype(jnp.int32)


def _select_rows(table, pos):
    out = jnp.zeros(pos.shape, table.dtype)
    for i in range(PEER_TOPK):
        out = jnp.where(pos == i, table[i:i + 1, :], out)
    return out


def _pair_candidates():
    blocks = [(0, 1, 0, PEER_TOPK)] + [(i, i + 1, 0, SUBLANES) for i in range(1, SUBLANES)]
    blocks.append((SUBLANES, PEER_TOPK, 0, 1))
    codes = np.concatenate([np.array([i * PEER_TOPK + j for i in range(i0, i1) for j in range(j0, j1)])
                            for i0, i1, j0, j1 in blocks])
    return blocks, codes.astype(np.float32)[:, None]


def _peer_topk_body(sc_ref, pair_code_ref, idx_ref, g_ref):
    blocks, _ = _pair_candidates()
    tb = idx_ref.shape[1]
    key_code = lax.broadcasted_iota(jnp.int32, (PEER_N_KEYS, tb), 0).astype(F32)
    pair_code = jnp.broadcast_to(pair_code_ref[...], (pair_code_ref.shape[0], tb))
    experts, gates = [], []
    for h in range(PEER_HEADS):
        s0, i0 = _top16(sc_ref[2 * h], key_code)
        s1, i1 = _top16(sc_ref[2 * h + 1], key_code)
        cand = jnp.concatenate([s0[a0:a1, :] + s1[b0:b1, :] for a0, a1, b0, b1 in blocks], axis=0)
        best, flat = _top16(cand, pair_code)
        expert = (_select_rows(i0, jnp.right_shift(flat, 4)) * PEER_N_KEYS
                  + _select_rows(i1, jnp.bitwise_and(flat, PEER_TOPK - 1)))
        e = jnp.exp(best - best[0:1, :])
        odd = jnp.bitwise_and(lax.broadcasted_iota(jnp.int32, expert.shape, 0), 1)
        experts.append(expert * ROW_SUBLANES + (SUBLANES - ROW_SUBLANES * odd))
        gates.append(e / jnp.sum(e, axis=0, keepdims=True))
    idx_ref[...] = jnp.concatenate(experts, axis=0)
    g_ref[...] = jnp.concatenate(gates, axis=0).T


def _peer_topk(scores, *, tb, interpret):
    nhp, nk, n = scores.shape
    pair_code = jnp.asarray(_pair_candidates()[1])
    return pl.pallas_call(
        _peer_topk_body,
        grid=(n // tb,),
        in_specs=[pl.BlockSpec((nhp, nk, tb), lambda i: (0, 0, i)), _resident(pair_code.shape)],
        out_specs=[pl.BlockSpec((PEER_SLOTS, tb), lambda i: (0, i)), pl.BlockSpec((tb, PEER_SLOTS), lambda i: (i, 0))],
        out_shape=[jax.ShapeDtypeStruct((PEER_SLOTS, n), jnp.int32), jax.ShapeDtypeStruct((n, PEER_SLOTS), F32)],
        compiler_params=_cparams(("parallel",)),
        name="peer_topk",
        interpret=interpret,
    )(scores, pair_code)


def _pack_table(t):
    e = t.shape[0]
    bits = lax.bitcast_convert_type(t.astype(BF16), jnp.uint16).astype(jnp.uint32)
    bits = bits.reshape(e, 2, ROW_SUBLANES, LANES)
    packed = (bits[:, 0] | (bits[:, 1] << 16)).reshape(e * ROW_SUBLANES, LANES)
    pad = jnp.zeros((SUBLANES, LANES), jnp.uint32)
    return jnp.concatenate([pad, packed, pad], axis=0)


def _expert_rows(idx_scr, tab_ref, t):
    low = lax.broadcasted_iota(jnp.int32, (SUBLANES, LANES), 0) < ROW_SUBLANES
    tiles = [tab_ref[pl.ds(pl.multiple_of(idx_scr.at[k][t], ROW_SUBLANES), SUBLANES), :] for k in range(PEER_SLOTS)]
    pairs = [jnp.where(low, tiles[2 * m], tiles[2 * m + 1]) for m in range(PEER_SLOTS // 2)]
    return pltpu.bitcast(jnp.concatenate(pairs, axis=0), BF16)


def _token_blocks(idx_hbm, idx_a, idx_b, sem, per_token, tb):
    i = pl.program_id(0)

    def copy(blk, scr, s):
        return pltpu.make_async_copy(idx_hbm.at[:, pl.ds(pl.multiple_of(blk * tb, LANES), tb)], scr, sem.at[s])

    def run_block(idx_scr, base):
        def body(j, carry):
            for u in range(PEER_TOKENS_PER_STEP):
                per_token(idx_scr, PEER_TOKENS_PER_STEP * j + u, base)
            return carry

        lax.fori_loop(0, tb // PEER_TOKENS_PER_STEP, body, 0)

    @pl.when(i == 0)
    def _():
        copy(0, idx_a, 0).start()

    copy(2 * i, idx_a, 0).wait()
    copy(2 * i + 1, idx_b, 1).start()
    run_block(idx_a, 0)
    copy(2 * i + 1, idx_b, 1).wait()

    @pl.when(i + 1 < pl.num_programs(0))
    def _():
        copy(2 * i + 2, idx_a, 0).start()

    run_block(idx_b, tb)


def _piece_mask():
    r = lax.broadcasted_iota(jnp.int32, (ROW_PIECES, PEER_SLOTS * ROW_PIECES), 0)
    c = lax.broadcasted_iota(jnp.int32, (ROW_PIECES, PEER_SLOTS * ROW_PIECES), 1)
    return (c % ROW_PIECES == r).astype(F32)


def _peer_u_body(idx_hbm, x8_ref, g_ref, tab_ref, sel_ref, selt_ref, w_ref, idx_a, idx_b, sem, r_scr, *, tb):
    dmask = _piece_mask()

    def per_token(idx_scr, t, base):
        x8 = x8_ref[pl.ds(pl.multiple_of((base + t) * ROW_PIECES, ROW_PIECES), ROW_PIECES), :].astype(BF16)
        r = _dot_nt(x8, _expert_rows(idx_scr, tab_ref, t))
        r_scr[pl.ds(base + t, 1), :] = jnp.sum(r * dmask, axis=0, keepdims=True)

    _token_blocks(idx_hbm, idx_a, idx_b, sem, per_token, tb)
    hi, lo = _split2(r_scr[...])
    s = _dot(hi, sel_ref[...]) + _dot(lo, sel_ref[...])
    act = 0.5 * s * (1.0 + lax.erf(s * (2.0 ** -0.5)))
    w = (g_ref[...] * act).astype(BF16)
    w_ref[...] = _dot(w, selt_ref[...])


def _peer_v_body(idx_hbm, w_ref, x1_ref, tab_ref, o_ref, idx_a, idx_b, sem, o8_scr, *, tb):
    dmask = _piece_mask()

    def per_token(idx_scr, t, base):
        lhs = (w_ref[pl.ds(base + t, 1), :] * dmask).astype(BF16)
        o8_scr[pl.ds(pl.multiple_of((base + t) * ROW_PIECES, ROW_PIECES), ROW_PIECES), :] = _dot(
            lhs, _expert_rows(idx_scr, tab_ref, t))

    _token_blocks(idx_hbm, idx_a, idx_b, sem, per_token, tb)
    for j in range(ROW_PIECES):
        cs = slice(_piece_offset(j), _piece_offset(j) + LANES)
        o_ref[:, cs] = x1_ref[:, cs] + o8_scr[pl.ds(j, 2 * tb, stride=ROW_PIECES), :]


def _idx_scratch(tb):
    return [pltpu.SMEM((PEER_SLOTS, tb), jnp.int32), pltpu.SMEM((PEER_SLOTS, tb), jnp.int32),
            pltpu.SemaphoreType.DMA((2,))]


def _peer_u(idx_t, x8, gates, table, *, tb, interpret):
    n = idx_t.shape[1]
    ts = 2 * tb
    sel_np = (np.arange(PEER_SLOTS * ROW_PIECES)[:, None] // ROW_PIECES == np.arange(PEER_SLOTS)[None, :])
    sel = jnp.asarray(sel_np.astype(np.float32), dtype=BF16)
    return pl.pallas_call(
        functools.partial(_peer_u_body, tb=tb),
        grid=(n // ts,),
        in_specs=[pl.BlockSpec(memory_space=pl.ANY),
                  pl.BlockSpec((ts * ROW_PIECES, LANES), lambda i: (i, 0)),
                  pl.BlockSpec((ts, PEER_SLOTS), lambda i: (i, 0)),
                  _resident(table.shape), _resident(sel.shape), _resident(sel.T.shape)],
        out_specs=pl.BlockSpec((ts, PEER_SLOTS * ROW_PIECES), lambda i: (i, 0)),
        out_shape=jax.ShapeDtypeStruct((n, PEER_SLOTS * ROW_PIECES), F32),
        scratch_shapes=_idx_scratch(tb) + [pltpu.VMEM((ts, PEER_SLOTS * ROW_PIECES), F32)],
        compiler_params=_cparams(("arbitrary",)),
        name="peer_u",
        interpret=interpret,
    )(idx_t, x8, gates, table, sel, sel.T)


def _peer_v(idx_t, w_exp, x1, table, *, tb, interpret):
    n = idx_t.shape[1]
    ts = 2 * tb
    return pl.pallas_call(
        functools.partial(_peer_v_body, tb=tb),
        grid=(n // ts,),
        in_specs=[pl.BlockSpec(memory_space=pl.ANY),
                  pl.BlockSpec((ts, PEER_SLOTS * ROW_PIECES), lambda i: (i, 0)),
                  pl.BlockSpec((ts, D_MODEL), lambda i: (i, 0)),
                  _resident(table.shape)],
        out_specs=pl.BlockSpec((ts, D_MODEL), lambda i: (i, 0)),
        out_shape=jax.ShapeDtypeStruct((n, D_MODEL), F32),
        scratch_shapes=_idx_scratch(tb) + [pltpu.VMEM((ts * ROW_PIECES, LANES), F32)],
        compiler_params=_cparams(("arbitrary",)),
        name="peer_v",
        interpret=interpret,
    )(idx_t, w_exp, x1, table)


def _layer(x2, mem2, p, *, layer, batch, seq, mem_len, tm, attn_blk_q, attn_blk_k, peer_tb, interpret):
    lam_init = 0.8 - 0.6 * math.exp(-0.3 * layer)
    (daq, dak, dav, glq, glk, glv, glvt, glr, gla, xaq, gates) = _in_proj(
        x2, p["mix_norm_w"], p["w_in"], p["da_q_norm_w"], p["da_k_norm_w"], p["xa_q_norm_w"], p["gla_w_gate"],
        p["gla_b_gate"], seq=seq, tm=tm, interpret=interpret)
    oda = _diff_attn(daq, dak, dav, p["da_lambda"], p["da_out_norm_w"], batch=batch, seq=seq, blk_q=attn_blk_q,
                     blk_k=attn_blk_k, lam_init=lam_init, interpret=interpret)
    ogla = _gla(glq, glk, gla, glv, glvt, glr, p["gla_out_norm_w"], batch=batch, seq=seq, interpret=interpret)
    mk, mv = _mem_kv(mem2, p["mem_norm_w"], p["w_mem_kv"], p["xa_k_norm_w"], batch=batch, mem_len=mem_len,
                     interpret=interpret)
    x1, x8, scores = _merge(x2, oda, ogla, xaq, gates, mk, mv, p["w_br_da"], p["w_br_gla"], p["w_br_xa"], p["w_out"],
                             p["ffn_norm_w"], p["peer_w_q"], p["peer_sub_keys"], seq=seq, mem_len=mem_len, tm=tm,
                             interpret=interpret)
    idx_t, peer_gates = _peer_topk(scores, tb=peer_tb, interpret=interpret)
    w_exp = _peer_u(idx_t, x8, peer_gates, _pack_table(p["peer_u"]), tb=peer_tb, interpret=interpret)
    return _peer_v(idx_t, w_exp, x1, _pack_table(p["peer_v"]), tb=peer_tb, interpret=interpret)


_PARAM_NAMES = ("mix_norm_w", "w_in", "da_q_norm_w", "da_k_norm_w", "da_lambda", "da_out_norm_w", "gla_w_gate",
                "gla_b_gate", "gla_out_norm_w", "mem_norm_w", "w_mem_kv", "xa_q_norm_w", "xa_k_norm_w", "w_br_da",
                "w_br_gla", "w_br_xa", "w_out", "ffn_norm_w", "peer_w_q", "peer_sub_keys", "peer_u", "peer_v")


def _forward(x, mem, params, *, tm=256, attn_blk_q=512, attn_blk_k=1024, peer_tb=128, interpret=False):
    batch, seq, d = x.shape
    mem_len = mem.shape[1]
    x2 = x.reshape(batch * seq, d)
    mem2 = mem.reshape(batch * mem_len, d)
    depth = params["w_in"].shape[0]
    for layer in range(depth):
        p = {name: params[name][layer] for name in _PARAM_NAMES}
        x2 = _layer(x2, mem2, p, layer=layer, batch=batch, seq=seq, mem_len=mem_len, tm=tm, attn_blk_q=attn_blk_q,
                    attn_blk_k=attn_blk_k, peer_tb=peer_tb, interpret=interpret)
    return x2.reshape(batch, seq, d)


def kernel(x, mem, mix_norm_w, w_in, da_q_norm_w, da_k_norm_w, da_lambda, da_out_norm_w, gla_w_gate, gla_b_gate,
           gla_out_norm_w, mem_norm_w, w_mem_kv, xa_q_norm_w, xa_k_norm_w, w_br_da, w_br_gla, w_br_xa, w_out,
           ffn_norm_w, peer_w_q, peer_sub_keys, peer_u, peer_v):
    params = dict(zip(_PARAM_NAMES, (mix_norm_w, w_in, da_q_norm_w, da_k_norm_w, da_lambda, da_out_norm_w, gla_w_gate,
                                     gla_b_gate, gla_out_norm_w, mem_norm_w, w_mem_kv, xa_q_norm_w, xa_k_norm_w,
                                     w_br_da, w_br_gla, w_br_xa, w_out, ffn_norm_w, peer_w_q, peer_sub_keys, peer_u,
                                     peer_v)))
    return _forward(x, mem, params)
```

```python
import functools
import math

import jax
import jax.numpy as jnp
import numpy as np
from jax import lax
from jax.experimental import pallas as pl
from jax.experimental.pallas import tpu as pltpu

F32 = jnp.float32
BF16 = jnp.bfloat16

LANES = 128
SUBLANES = 8
VMEM_LIMIT_BYTES = 56 * 1024 * 1024

D_MODEL = 1024
DA_HEADS = 4
DA_HEAD_DIM = 64
DA_V_DIM = 2 * DA_HEAD_DIM
GLA_HEADS = 4
GLA_DK = 64
GLA_DV = 128
GLA_GATE_RANK = 16
GLA_TAU = 16.0
XA_HEADS = 4
XA_HEAD_DIM = 128
N_BRANCH = 3
PEER_HEADS = 8
PEER_N_KEYS = 128
PEER_QDIM = 256
PEER_TOPK = 16
RMS_EPS = 1e-6
LOG2E = math.log2(math.e)

DA_W = DA_HEADS * 2 * DA_HEAD_DIM
GLA_QW = GLA_HEADS * GLA_DK
GLA_VW = GLA_HEADS * GLA_DV
XA_W = XA_HEADS * XA_HEAD_DIM
GATE_W = N_BRANCH * D_MODEL
PEER_SLOTS = PEER_HEADS * PEER_TOPK

C_DAQ = 0
C_DAK = C_DAQ + DA_W
C_DAV = C_DAK + DA_W
C_GLQ = C_DAV + DA_W
C_GLK = C_GLQ + GLA_QW
C_GLV = C_GLK + GLA_QW
C_GLR = C_GLV + GLA_VW
C_XAQ = C_GLR + GLA_VW
C_GATE = C_XAQ + XA_W
C_GLG = C_GATE + GATE_W
C_END = C_GLG + LANES

GLA_CHUNK = 128
GLA_LEVELS = 7
GLA_CHUNKS_PER_STEP = 2

ROW_SUBLANES = 4
ROW_PIECES = 2 * ROW_SUBLANES
PEER_TOKENS_PER_STEP = 32


def _cparams(sem):
    return pltpu.CompilerParams(dimension_semantics=sem, vmem_limit_bytes=VMEM_LIMIT_BYTES)


def _resident(shape):
    nd = len(shape)
    return pl.BlockSpec(shape, lambda *_: (0,) * nd, pipeline_mode=pl.Buffered(1))


def _dot(a, b):
    return jnp.dot(a, b, preferred_element_type=F32)


def _dot_nt(a, b):
    return lax.dot_general(a, b, (((1,), (1,)), ((), ())), preferred_element_type=F32)


def _split2(t):
    hi = t.astype(BF16)
    lo = (t - hi.astype(F32)).astype(BF16)
    return hi, lo


def _group_sumsq(t, bd_ref):
    hi, lo = _split2(t * t)
    bd = bd_ref[...]
    return _dot(hi, bd) + _dot(lo, bd)


def _in_proj_body(x_ref, nw_ref, w_ref, wvt_ref, qw_ref, kw_ref, xw_ref, wg_ref, bg_ref, bd64_ref, bd128_ref,
                  daq_ref, dak_ref, dav_ref, glq_ref, glk_ref, glv_ref, glvt_ref, glr_ref, gla_ref, xaq_ref,
                  gate_ref, *, seq):
    x = x_ref[...]
    ms = jnp.mean(x * x, axis=-1, keepdims=True)
    xn = (x * lax.rsqrt(ms + RMS_EPS) * nw_ref[...]).astype(BF16)

    def proj(c0, width):
        return _dot(xn, w_ref[:, c0:c0 + width])

    q = proj(C_DAQ, DA_W)
    q = q * lax.rsqrt(_group_sumsq(q, bd64_ref) * (1.0 / DA_HEAD_DIM) + RMS_EPS) * qw_ref[...]
    k = proj(C_DAK, DA_W)
    k = k * lax.rsqrt(_group_sumsq(k, bd64_ref) * (1.0 / DA_HEAD_DIM) + RMS_EPS) * kw_ref[...]
    tm = x.shape[0]
    lane = lax.broadcasted_iota(jnp.int32, (1, LANES), 1)
    pos = (pl.program_id(0) * tm + lax.broadcasted_iota(jnp.int32, (tm, 1), 0)) % seq
    pos = pos.astype(F32)
    for h in range(DA_HEADS):
        hs = slice(h * DA_V_DIM, (h + 1) * DA_V_DIM)
        b0 = pos * (LOG2E * 2.0 ** (-8.0 * (h + 1) / DA_HEADS))
        p0 = b0.astype(BF16).astype(F32)
        b1 = b0 - p0
        p1 = b1.astype(BF16).astype(F32)
        p2 = b1 - p1
        for c in range(2):
            own = (lane < DA_HEAD_DIM) if c == 0 else (lane >= DA_HEAD_DIM)
            first = DA_HEAD_DIM * (1 - c)
            cs = slice((2 * h + c) * LANES, (2 * h + c + 1) * LANES)
            ones = ((lane >= first) & (lane < first + 3)).astype(F32)
            daq_ref[:, cs] = jnp.where(own, q[:, hs], ones).astype(BF16)
            feat = jnp.where(lane == first, p0, jnp.where(lane == first + 1, p1, jnp.where(lane == first + 2, p2, 0.0)))
            dak_ref[:, cs] = jnp.where(own, k[:, hs], feat).astype(BF16)
    dav_ref[...] = proj(C_DAV, DA_W).astype(BF16)

    glq_ref[...] = proj(C_GLQ, GLA_QW) * (GLA_DK ** -0.5)
    glk_ref[...] = proj(C_GLK, GLA_QW)
    glv_ref[...] = proj(C_GLV, GLA_VW).astype(BF16)
    glvt_ref[...] = _dot_nt(wvt_ref[...], xn).astype(BF16)
    r = proj(C_GLR, GLA_VW)
    glr_ref[...] = (r * jax.nn.sigmoid(r)).astype(BF16)

    g = proj(C_GLG, LANES)
    z = jnp.dot(g, wg_ref[...], preferred_element_type=F32, precision=lax.Precision.HIGHEST) + bg_ref[...]
    log_sig = jnp.minimum(z, 0.0) - jnp.log1p(jnp.exp(-jnp.abs(z)))
    gla_ref[...] = log_sig * (1.0 / GLA_TAU)

    xq = proj(C_XAQ, XA_W)
    xaq_ref[...] = (xq * lax.rsqrt(_group_sumsq(xq, bd128_ref) * (1.0 / XA_HEAD_DIM) + RMS_EPS) * xw_ref[...]).astype(BF16)

    for c in range(0, GATE_W, 512):
        gate_ref[:, c:c + 512] = jax.nn.sigmoid(proj(C_GATE + c, 512)).astype(BF16)


def _block_diag_ones(width, group):
    idx = np.arange(width) // group
    return jnp.asarray((idx[:, None] == idx[None, :]).astype(np.float32), dtype=BF16)


def _in_proj(x2, mix_norm_w, w_in, da_q_norm_w, da_k_norm_w, xa_q_norm_w, gla_w_gate, gla_b_gate, *, seq, tm,
             interpret):
    n = x2.shape[0]
    c_glg_src = C_GLR + GLA_VW
    w = jnp.concatenate(
        [w_in[:, :c_glg_src], w_in[:, c_glg_src + GLA_GATE_RANK:], w_in[:, c_glg_src:c_glg_src + GLA_GATE_RANK],
         jnp.zeros((D_MODEL, LANES - GLA_GATE_RANK), w_in.dtype)], axis=1).astype(BF16)
    wvt = w_in[:, C_GLV:C_GLV + GLA_VW].T.astype(BF16)
    qw = jnp.tile(da_q_norm_w, 2 * DA_HEADS)[None, :] * (DA_HEAD_DIM ** -0.5 * LOG2E)
    kw = jnp.tile(da_k_norm_w, 2 * DA_HEADS)[None, :]
    xw = jnp.tile(xa_q_norm_w, XA_HEADS)[None, :] * (XA_HEAD_DIM ** -0.5)
    wg = jnp.zeros((LANES, GLA_QW), F32).at[:GLA_GATE_RANK].set(gla_w_gate)
    bg = gla_b_gate[None, :]
    row = lambda width: pl.BlockSpec((tm, width), lambda i: (i, 0))
    out_widths = [(2 * DA_W, BF16), (2 * DA_W, BF16), (DA_W, BF16), (GLA_QW, F32), (GLA_QW, F32), (GLA_VW, BF16)]
    out_shape = [jax.ShapeDtypeStruct((n, wd), dt) for wd, dt in out_widths]
    out_specs = [row(wd) for wd, _ in out_widths]
    out_shape.append(jax.ShapeDtypeStruct((GLA_VW, n), BF16))
    out_specs.append(pl.BlockSpec((GLA_VW, tm), lambda i: (0, i)))
    for wd, dt in [(GLA_VW, BF16), (GLA_QW, F32), (XA_W, BF16), (GATE_W, BF16)]:
        out_shape.append(jax.ShapeDtypeStruct((n, wd), dt))
        out_specs.append(row(wd))
    return pl.pallas_call(
        functools.partial(_in_proj_body, seq=seq),
        grid=(n // tm,),
        in_specs=[row(D_MODEL), _resident((1, D_MODEL)), _resident((D_MODEL, C_END)), _resident((GLA_VW, D_MODEL)),
                  _resident((1, DA_W)), _resident((1, DA_W)), _resident((1, XA_W)), _resident((LANES, GLA_QW)),
                  _resident((1, GLA_QW)), _resident((DA_W, DA_W)), _resident((XA_W, XA_W))],
        out_specs=out_specs,
        out_shape=out_shape,
        compiler_params=_cparams(("parallel",)),
        name="in_proj",
        interpret=interpret,
    )(x2, mix_norm_w[None, :], w, wvt, qw, kw, xw, wg, bg, _block_diag_ones(DA_W, DA_HEAD_DIM),
      _block_diag_ones(XA_W, XA_HEAD_DIM))


def _diff_attn_body(qi_tab, kj_tab, q_ref, k_ref, v_ref, lam_ref, ow_ref, o_ref, m_scr, acc_scr, *, blk_q, blk_k,
                    lam_init):
    p = pl.program_id(1)
    qi = qi_tab[p]
    kj = kj_tab[p]
    last = (qi * blk_q) // blk_k

    @pl.when(kj == 0)
    def _():
        m_scr[...] = jnp.full(m_scr.shape, -jnp.inf, F32)
        acc_scr[...] = jnp.zeros(acc_scr.shape, F32)

    def component(h, c, causal):
        v = v_ref[:, h * DA_V_DIM:(h + 1) * DA_V_DIM]
        v_ones = jnp.concatenate([v, jnp.ones_like(v)], axis=1)
        cs = slice((2 * h + c) * LANES, (2 * h + c + 1) * LANES)
        s = _dot_nt(q_ref[:, cs], k_ref[:, cs])
        if causal:
            row = lax.broadcasted_iota(jnp.int32, (blk_q, 1), 0) + qi * blk_q
            col = lax.broadcasted_iota(jnp.int32, (1, blk_k), 1) + kj * blk_k
            s = jnp.where(col <= row, s, -jnp.inf)
        i = 2 * h + c
        m_prev = m_scr[i]
        m_new = jnp.maximum(m_prev, jnp.max(s, axis=-1, keepdims=True))
        acc_scr[i] = jnp.exp2(m_prev - m_new) * acc_scr[i] + _dot(jnp.exp2(s - m_new).astype(BF16), v_ones)
        m_scr[i] = m_new

    @pl.when(kj < last)
    def _():
        for h in range(DA_HEADS):
            component(h, 0, False)
            component(h, 1, False)

    @pl.when(kj == last)
    def _():
        lv = lam_ref[...]
        lam = (jnp.exp(jnp.sum(lv[0:1] * lv[1:2], axis=-1, keepdims=True))
               - jnp.exp(jnp.sum(lv[2:3] * lv[3:4], axis=-1, keepdims=True)) + lam_init)
        for h in range(DA_HEADS):
            component(h, 0, True)
            component(h, 1, True)
            a0 = acc_scr[2 * h]
            a1 = acc_scr[2 * h + 1]
            o = a0[:, :DA_V_DIM] / a0[:, DA_V_DIM:] - lam * (a1[:, :DA_V_DIM] / a1[:, DA_V_DIM:])
            ms = jnp.mean(o * o, axis=-1, keepdims=True)
            o_ref[:, h * DA_V_DIM:(h + 1) * DA_V_DIM] = (
                o * lax.rsqrt(ms + RMS_EPS) * ow_ref[...] * (1.0 - lam_init)).astype(BF16)


def _diff_attn(daq, dak, dav, da_lambda, da_out_norm_w, *, batch, seq, blk_q, blk_k, lam_init, interpret):
    assert blk_k % blk_q == 0 and seq % blk_k == 0
    nq = seq // blk_q
    nk = seq // blk_k
    pairs = [(qi, kj) for qi in range(nq) for kj in range((qi * blk_q) // blk_k + 1)]
    qi_tab = jnp.asarray([p[0] for p in pairs], jnp.int32)
    kj_tab = jnp.asarray([p[1] for p in pairs], jnp.int32)
    qmap = lambda b, p, qt, kt: (b * nq + qt[p], 0)
    kmap = lambda b, p, qt, kt: (b * nk + kt[p], 0)
    n_comp = 2 * DA_HEADS
    grid_spec = pltpu.PrefetchScalarGridSpec(
        num_scalar_prefetch=2,
        grid=(batch, len(pairs)),
        in_specs=[pl.BlockSpec((blk_q, n_comp * LANES), qmap), pl.BlockSpec((blk_k, n_comp * LANES), kmap),
                  pl.BlockSpec((blk_k, DA_HEADS * DA_V_DIM), kmap),
                  pl.BlockSpec((4, DA_HEAD_DIM), lambda b, p, qt, kt: (0, 0)),
                  pl.BlockSpec((1, DA_V_DIM), lambda b, p, qt, kt: (0, 0))],
        out_specs=pl.BlockSpec((blk_q, DA_HEADS * DA_V_DIM), qmap),
        scratch_shapes=[pltpu.VMEM((n_comp, blk_q, 1), F32), pltpu.VMEM((n_comp, blk_q, 2 * DA_V_DIM), F32)],
    )
    return pl.pallas_call(
        functools.partial(_diff_attn_body, blk_q=blk_q, blk_k=blk_k, lam_init=lam_init),
        grid_spec=grid_spec,
        out_shape=jax.ShapeDtypeStruct((batch * seq, DA_HEADS * DA_V_DIM), BF16),
        compiler_params=_cparams(("parallel", "arbitrary")),
        name="diff_attn",
        interpret=interpret,
    )(qi_tab, kj_tab, daq, dak, dav, da_lambda, da_out_norm_w[None, :])


def _gla_constants():
    c = GLA_CHUNK
    t = np.arange(c)[:, None]
    u = np.arange(c)[None, :]
    lmats, masks = [], []
    for lev in range(GLA_LEVELS):
        m = 1 << lev
        second = (t % (2 * m)) >= m
        first = ~second
        bnd = (t // m) * m
        lmats.append(second & (u >= bnd) & (u <= t))
    for lev in range(GLA_LEVELS):
        m = 1 << lev
        first = (t % (2 * m)) < m
        end = (t // m) * m + m - 1
        lmats.append(first & (u > t) & (u <= end))
    lmats.append(u <= t)
    lmats.append(u > t)
    for lev in range(GLA_LEVELS):
        m = 1 << lev
        s = np.arange(c)[None, :]
        masks.append(((t // (2 * m)) == (s // (2 * m))) & ((t % (2 * m)) >= m) & ((s % (2 * m)) < m))
    masks.append(t == np.arange(c)[None, :])
    lall = jnp.asarray(np.concatenate(lmats, axis=0).astype(np.float32), dtype=BF16)
    mall = jnp.asarray(np.stack(masks, axis=0).astype(np.float32))
    return lall, mall


def _gla_body(q_ref, k_ref, la_ref, v_ref, vt_ref, r_ref, lall_ref, mall_ref, ow_ref, o_ref, state_scr):
    c = GLA_CHUNK

    @pl.when(pl.program_id(1) == 0)
    def _():
        state_scr[...] = jnp.zeros(state_scr.shape, F32)

    lall = lall_ref[...]
    lane = lax.broadcasted_iota(jnp.int32, (1, GLA_QW), 1)
    head_masks = [(lane // GLA_DK == h).astype(F32) for h in range(GLA_HEADS)]

    def intra(rs):
        q = q_ref[rs, :]
        k = k_ref[rs, :]
        g = la_ref[rs, :]
        g1 = g.astype(BF16)
        r1 = g - g1.astype(F32)
        g2 = r1.astype(BF16)
        g3 = (r1 - g2.astype(F32)).astype(BF16)
        e_all = jnp.exp(_dot(lall, g1) + _dot(lall, g2) + _dot(lall, g3))

        def rows(i):
            return e_all[i * c:(i + 1) * c]

        a = [jnp.zeros((c, c), F32) for _ in range(GLA_HEADS)]
        for lev in range(GLA_LEVELS + 1):
            if lev < GLA_LEVELS:
                ql = q * rows(lev)
                kl = (k * rows(GLA_LEVELS + lev)).astype(BF16)
            else:
                ql = q
                kl = k.astype(BF16)
            mask = mall_ref[lev]
            for h in range(GLA_HEADS):
                a[h] = a[h] + mask * _dot_nt((ql * head_masks[h]).astype(BF16), kl)
        e_b = rows(2 * GLA_LEVELS)
        return a, q * e_b, (k * rows(2 * GLA_LEVELS + 1)).astype(BF16), e_b[c - 1:c, :]

    chunks = [slice(i * c, (i + 1) * c) for i in range(GLA_CHUNKS_PER_STEP)]
    parts = [intra(rs) for rs in chunks]
    for rs, (a, q_dec, k_dec, chunk_decay) in zip(chunks, parts):
        for h in range(GLA_HEADS):
            vs = slice(h * GLA_DV, (h + 1) * GLA_DV)
            st = state_scr[h]
            o = (_dot(a[h].astype(BF16), v_ref[rs, vs])
                 + _dot_nt((q_dec * head_masks[h]).astype(BF16), st.astype(BF16)))
            state_scr[h] = st * chunk_decay + _dot(vt_ref[vs, rs], k_dec)
            ms = jnp.mean(o * o, axis=-1, keepdims=True)
            o_ref[rs, vs] = (o * lax.rsqrt(ms + RMS_EPS) * ow_ref[...] * r_ref[rs, vs].astype(F32)).astype(BF16)


def _gla(glq, glk, gla, glv, glvt, glr, gla_out_norm_w, *, batch, seq, interpret):
    rows_per_step = GLA_CHUNK * GLA_CHUNKS_PER_STEP
    ns = seq // rows_per_step
    lall, mall = _gla_constants()
    row = lambda width: pl.BlockSpec((rows_per_step, width), lambda b, i: (b * ns + i, 0))
    return pl.pallas_call(
        _gla_body,
        grid=(batch, ns),
        in_specs=[row(GLA_QW), row(GLA_QW), row(GLA_QW), row(GLA_VW),
                  pl.BlockSpec((GLA_VW, rows_per_step), lambda b, i: (0, b * ns + i)), row(GLA_VW),
                  _resident(lall.shape), _resident(mall.shape), _resident((1, GLA_DV))],
        out_specs=row(GLA_VW),
        out_shape=jax.ShapeDtypeStruct((batch * seq, GLA_VW), BF16),
        scratch_shapes=[pltpu.VMEM((GLA_HEADS, GLA_DV, GLA_QW), F32)],
        compiler_params=_cparams(("parallel", "arbitrary")),
        name="gla",
        interpret=interpret,
    )(glq, glk, gla, glv, glvt, glr, lall, mall, gla_out_norm_w[None, :])


def _mem_kv_body(mem_ref, nw_ref, w_ref, kw_ref, bd_ref, mk_ref, mv_ref):
    x = mem_ref[...]
    ms = jnp.mean(x * x, axis=-1, keepdims=True)
    xn = (x * lax.rsqrt(ms + RMS_EPS) * nw_ref[...]).astype(BF16)
    k = _dot(xn, w_ref[:, :XA_W])
    mk_ref[...] = (k * lax.rsqrt(_group_sumsq(k, bd_ref) * (1.0 / XA_HEAD_DIM) + RMS_EPS) * kw_ref[...]).astype(BF16)
    mv_ref[...] = _dot(xn, w_ref[:, XA_W:]).astype(BF16)


def _mem_kv(mem2, mem_norm_w, w_mem_kv, xa_k_norm_w, *, batch, mem_len, interpret):
    blk = pl.BlockSpec((mem_len, XA_W), lambda b: (b, 0))
    return pl.pallas_call(
        _mem_kv_body,
        grid=(batch,),
        in_specs=[pl.BlockSpec((mem_len, D_MODEL), lambda b: (b, 0)), _resident((1, D_MODEL)),
                  _resident((D_MODEL, 2 * XA_W)), _resident((1, XA_W)), _resident((XA_W, XA_W))],
        out_specs=[blk, blk],
        out_shape=[jax.ShapeDtypeStruct((batch * mem_len, XA_W), BF16)] * 2,
        compiler_params=_cparams(("parallel",)),
        name="mem_kv",
        interpret=interpret,
    )(mem2, mem_norm_w[None, :], w_mem_kv.astype(BF16), jnp.tile(xa_k_norm_w, XA_HEADS)[None, :],
      _block_diag_ones(XA_W, XA_HEAD_DIM))


def _piece_offset(j):
    return (j % 2) * (D_MODEL // 2) + (j // 2) * LANES


def _merge_body(x_ref, oda_ref, ogla_ref, xaq_ref, gate_ref, mk_ref, mv_ref, wda_ref, wgla_ref, wxa_ref, wout_ref,
                fw_ref, wq_ref, keys_ref, x1_ref, x8_ref, sc_ref):
    tm = x_ref.shape[0]
    br_xa = None
    for h in range(XA_HEADS):
        hs = slice(h * XA_HEAD_DIM, (h + 1) * XA_HEAD_DIM)
        s = _dot_nt(xaq_ref[:, hs], mk_ref[:, hs])
        s = s - jnp.max(s, axis=-1, keepdims=True)
        p = jnp.exp(s)
        p = p / jnp.sum(p, axis=-1, keepdims=True)
        o = _dot(p.astype(BF16), mv_ref[:, hs])
        t = _dot(o.astype(BF16), wxa_ref[hs, :])
        br_xa = t if br_xa is None else br_xa + t
    merged = (gate_ref[:, 0:D_MODEL].astype(F32) * _dot(oda_ref[...], wda_ref[...])
              + gate_ref[:, D_MODEL:2 * D_MODEL].astype(F32) * _dot(ogla_ref[...], wgla_ref[...])
              + gate_ref[:, 2 * D_MODEL:].astype(F32) * br_xa)
    x1 = x_ref[...] + _dot(merged.astype(BF16), wout_ref[...])
    x1_ref[...] = x1
    ms = jnp.mean(x1 * x1, axis=-1, keepdims=True)
    xn = x1 * lax.rsqrt(ms + RMS_EPS) * fw_ref[...]
    xnb = xn.astype(BF16)
    for j in range(ROW_PIECES):
        off = _piece_offset(j)
        x8_ref[pl.ds(j, tm, stride=ROW_PIECES), :] = xn[:, off:off + LANES]
    pq = _dot(xnb, wq_ref[...])
    half = PEER_QDIM // 2
    for hp in range(2 * PEER_HEADS):
        q_hi, q_lo = _split2(pq[:, hp * half:(hp + 1) * half])
        keys = keys_ref[hp % 2]
        sc_ref[hp] = _dot_nt(keys, q_hi) + _dot_nt(keys, q_lo)


def _merge(x2, oda, ogla, xaq, gates, mk, mv, w_br_da, w_br_gla, w_br_xa, w_out, ffn_norm_w, peer_w_q, peer_sub_keys,
           *, seq, mem_len, tm, interpret):
    n = x2.shape[0]
    spb = seq // tm
    row = lambda width: pl.BlockSpec((tm, width), lambda i: (i, 0))
    mem_blk = pl.BlockSpec((mem_len, XA_W), lambda i: (i // spb, 0))
    nhp = 2 * PEER_HEADS
    return pl.pallas_call(
        _merge_body,
        grid=(n // tm,),
        in_specs=[row(D_MODEL), row(DA_W), row(GLA_VW), row(XA_W), row(GATE_W), mem_blk, mem_blk,
                  _resident((DA_W, D_MODEL)), _resident((GLA_VW, D_MODEL)), _resident((XA_W, D_MODEL)),
                  _resident((D_MODEL, D_MODEL)), _resident((1, D_MODEL)),
                  _resident((D_MODEL, PEER_HEADS * PEER_QDIM)), _resident((2, PEER_N_KEYS, PEER_QDIM // 2))],
        out_specs=[row(D_MODEL), pl.BlockSpec((tm * ROW_PIECES, LANES), lambda i: (i, 0)),
                   pl.BlockSpec((nhp, PEER_N_KEYS, tm), lambda i: (0, 0, i))],
        out_shape=[jax.ShapeDtypeStruct((n, D_MODEL), F32), jax.ShapeDtypeStruct((n * ROW_PIECES, LANES), F32),
                   jax.ShapeDtypeStruct((nhp, PEER_N_KEYS, n), F32)],
        compiler_params=_cparams(("parallel",)),
        name="merge",
        interpret=interpret,
    )(x2, oda, ogla, xaq, gates, mk, mv, w_br_da.astype(BF16), w_br_gla.astype(BF16), w_br_xa.astype(BF16),
      w_out.astype(BF16), ffn_norm_w[None, :], peer_w_q.astype(BF16), peer_sub_keys.astype(BF16))


def _top16(x, code):
    vals, codes = [], []
    for _ in range(PEER_TOPK):
        m = jnp.max(x, axis=0, keepdims=True)
        win = jnp.min(jnp.where(x == m, code, jnp.inf), axis=0, keepdims=True)
        vals.append(m)
        codes.append(win)
        x = jnp.where(code == win, -jnp.inf, x)
    return jnp.concatenate(vals, axis=0), jnp.concatenate(codes, axis=0).astype(jnp.int32)


def _select_rows(table, pos):
    out = jnp.zeros(pos.shape, table.dtype)
    for i in range(PEER_TOPK):
        out = jnp.where(pos == i, table[i:i + 1, :], out)
    return out


def _pair_candidates():
    blocks = [(0, 1, 0, PEER_TOPK)] + [(i, i + 1, 0, SUBLANES) for i in range(1, SUBLANES)]
    blocks.append((SUBLANES, PEER_TOPK, 0, 1))
    codes = np.concatenate([np.array([i * PEER_TOPK + j for i in range(i0, i1) for j in range(j0, j1)])
                            for i0, i1, j0, j1 in blocks])
    return blocks, codes.astype(np.float32)[:, None]


def _peer_topk_body(sc_ref, pair_code_ref, idx_ref, g_ref):
    blocks, _ = _pair_candidates()
    tb = idx_ref.shape[1]
    key_code = lax.broadcasted_iota(jnp.int32, (PEER_N_KEYS, tb), 0).astype(F32)
    pair_code = jnp.broadcast_to(pair_code_ref[...], (pair_code_ref.shape[0], tb))
    experts, gates = [], []
    for h in range(PEER_HEADS):
        s0, i0 = _top16(sc_ref[2 * h], key_code)
        s1, i1 = _top16(sc_ref[2 * h + 1], key_code)
        cand = jnp.concatenate([s0[a0:a1, :] + s1[b0:b1, :] for a0, a1, b0, b1 in blocks], axis=0)
        best, flat = _top16(cand, pair_code)
        expert = (_select_rows(i0, jnp.right_shift(flat, 4)) * PEER_N_KEYS
                  + _select_rows(i1, jnp.bitwise_and(flat, PEER_TOPK - 1)))
        e = jnp.exp(best - best[0:1, :])
        odd = jnp.bitwise_and(lax.broadcasted_iota(jnp.int32, expert.shape, 0), 1)
        experts.append(expert * ROW_SUBLANES + (SUBLANES - ROW_SUBLANES * odd))
        gates.append(e / jnp.sum(e, axis=0, keepdims=True))
    idx_ref[...] = jnp.concatenate(experts, axis=0)
    g_ref[...] = jnp.concatenate(gates, axis=0).T


def _peer_topk(scores, *, tb, interpret):
    nhp, nk, n = scores.shape
    pair_code = jnp.asarray(_pair_candidates()[1])
    return pl.pallas_call(
        _peer_topk_body,
        grid=(n // tb,),
        in_specs=[pl.BlockSpec((nhp, nk, tb), lambda i: (0, 0, i)), _resident(pair_code.shape)],
        out_specs=[pl.BlockSpec((PEER_SLOTS, tb), lambda i: (0, i)), pl.BlockSpec((tb, PEER_SLOTS), lambda i: (i, 0))],
        out_shape=[jax.ShapeDtypeStruct((PEER_SLOTS, n), jnp.int32), jax.ShapeDtypeStruct((n, PEER_SLOTS), F32)],
        compiler_params=_cparams(("parallel",)),
        name="peer_topk",
        interpret=interpret,
    )(scores, pair_code)


def _pack_table(t):
    e = t.shape[0]
    bits = lax.bitcast_convert_type(t.astype(BF16), jnp.uint16).astype(jnp.uint32)
    bits = bits.reshape(e, 2, ROW_SUBLANES, LANES)
    packed = (bits[:, 0] | (bits[:, 1] << 16)).reshape(e * ROW_SUBLANES, LANES)
    pad = jnp.zeros((SUBLANES, LANES), jnp.uint32)
    return jnp.concatenate([pad, packed, pad], axis=0)


def _expert_rows(idx_scr, tab_ref, t):
    low = lax.broadcasted_iota(jnp.int32, (SUBLANES, LANES), 0) < ROW_SUBLANES
    tiles = [tab_ref[pl.ds(pl.multiple_of(idx_scr.at[k][t], ROW_SUBLANES), SUBLANES), :] for k in range(PEER_SLOTS)]
    pairs = [jnp.where(low, tiles[2 * m], tiles[2 * m + 1]) for m in range(PEER_SLOTS // 2)]
    return pltpu.bitcast(jnp.concatenate(pairs, axis=0), BF16)


def _token_blocks(idx_hbm, idx_a, idx_b, sem, per_token, tb):
    i = pl.program_id(0)

    def copy(blk, scr, s):
        return pltpu.make_async_copy(idx_hbm.at[:, pl.ds(pl.multiple_of(blk * tb, LANES), tb)], scr, sem.at[s])

    def run_block(idx_scr, base):
        def body(j, carry):
            for u in range(PEER_TOKENS_PER_STEP):
                per_token(idx_scr, PEER_TOKENS_PER_STEP * j + u, base)
            return carry

        lax.fori_loop(0, tb // PEER_TOKENS_PER_STEP, body, 0)

    @pl.when(i == 0)
    def _():
        copy(0, idx_a, 0).start()

    copy(2 * i, idx_a, 0).wait()
    copy(2 * i + 1, idx_b, 1).start()
    run_block(idx_a, 0)
    copy(2 * i + 1, idx_b, 1).wait()

    @pl.when(i + 1 < pl.num_programs(0))
    def _():
        copy(2 * i + 2, idx_a, 0).start()

    run_block(idx_b, tb)


def _piece_mask():
    r = lax.broadcasted_iota(jnp.int32, (ROW_PIECES, PEER_SLOTS * ROW_PIECES), 0)
    c = lax.broadcasted_iota(jnp.int32, (ROW_PIECES, PEER_SLOTS * ROW_PIECES), 1)
    return (c % ROW_PIECES == r).astype(F32)


def _peer_u_body(idx_hbm, x8_ref, g_ref, tab_ref, sel_ref, selt_ref, w_ref, idx_a, idx_b, sem, r_scr, *, tb):
    dmask = _piece_mask()

    def per_token(idx_scr, t, base):
        x8 = x8_ref[pl.ds(pl.multiple_of((base + t) * ROW_PIECES, ROW_PIECES), ROW_PIECES), :].astype(BF16)
        r = _dot_nt(x8, _expert_rows(idx_scr, tab_ref, t))
        r_scr[pl.ds(base + t, 1), :] = jnp.sum(r * dmask, axis=0, keepdims=True)

    _token_blocks(idx_hbm, idx_a, idx_b, sem, per_token, tb)
    hi, lo = _split2(r_scr[...])
    s = _dot(hi, sel_ref[...]) + _dot(lo, sel_ref[...])
    act = 0.5 * s * (1.0 + lax.erf(s * (2.0 ** -0.5)))
    w = (g_ref[...] * act).astype(BF16)
    w_ref[...] = _dot(w, selt_ref[...])


def _peer_v_body(idx_hbm, w_ref, x1_ref, tab_ref, o_ref, idx_a, idx_b, sem, o8_scr, *, tb):
    dmask = _piece_mask()

    def per_token(idx_scr, t, base):
        lhs = (w_ref[pl.ds(base + t, 1), :] * dmask).astype(BF16)
        o8_scr[pl.ds(pl.multiple_of((base + t) * ROW_PIECES, ROW_PIECES), ROW_PIECES), :] = _dot(
            lhs, _expert_rows(idx_scr, tab_ref, t))

    _token_blocks(idx_hbm, idx_a, idx_b, sem, per_token, tb)
    for j in range(ROW_PIECES):
        cs = slice(_piece_offset(j), _piece_offset(j) + LANES)
        o_ref[:, cs] = x1_ref[:, cs] + o8_scr[pl.ds(j, 2 * tb, stride=ROW_PIECES), :]


def _idx_scratch(tb):
    return [pltpu.SMEM((PEER_SLOTS, tb), jnp.int32), pltpu.SMEM((PEER_SLOTS, tb), jnp.int32),
            pltpu.SemaphoreType.DMA((2,))]


def _peer_u(idx_t, x8, gates, table, *, tb, interpret):
    n = idx_t.shape[1]
    ts = 2 * tb
    sel_np = (np.arange(PEER_SLOTS * ROW_PIECES)[:, None] // ROW_PIECES == np.arange(PEER_SLOTS)[None, :])
    sel = jnp.asarray(sel_np.astype(np.float32), dtype=BF16)
    return pl.pallas_call(
        functools.partial(_peer_u_body, tb=tb),
        grid=(n // ts,),
        in_specs=[pl.BlockSpec(memory_space=pl.ANY),
                  pl.BlockSpec((ts * ROW_PIECES, LANES), lambda i: (i, 0)),
                  pl.BlockSpec((ts, PEER_SLOTS), lambda i: (i, 0)),
                  _resident(table.shape), _resident(sel.shape), _resident(sel.T.shape)],
        out_specs=pl.BlockSpec((ts, PEER_SLOTS * ROW_PIECES), lambda i: (i, 0)),
        out_shape=jax.ShapeDtypeStruct((n, PEER_SLOTS * ROW_PIECES), F32),
        scratch_shapes=_idx_scratch(tb) + [pltpu.VMEM((ts, PEER_SLOTS * ROW_PIECES), F32)],
        compiler_params=_cparams(("arbitrary",)),
        name="peer_u",
        interpret=interpret,
    )(idx_t, x8, gates, table, sel, sel.T)


def _peer_v(idx_t, w_exp, x1, table, *, tb, interpret):
    n = idx_t.shape[1]
    ts = 2 * tb
    return pl.pallas_call(
        functools.partial(_peer_v_body, tb=tb),
        grid=(n // ts,),
        in_specs=[pl.BlockSpec(memory_space=pl.ANY),
                  pl.BlockSpec((ts, PEER_SLOTS * ROW_PIECES), lambda i: (i, 0)),
                  pl.BlockSpec((ts, D_MODEL), lambda i: (i, 0)),
                  _resident(table.shape)],
        out_specs=pl.BlockSpec((ts, D_MODEL), lambda i: (i, 0)),
        out_shape=jax.ShapeDtypeStruct((n, D_MODEL), F32),
        scratch_shapes=_idx_scratch(tb) + [pltpu.VMEM((ts * ROW_PIECES, LANES), F32)],
        compiler_params=_cparams(("arbitrary",)),
        name="peer_v",
        interpret=interpret,
    )(idx_t, w_exp, x1, table)


def _layer(x2, mem2, p, *, layer, batch, seq, mem_len, tm, attn_blk_q, attn_blk_k, peer_tb, interpret):
    lam_init = 0.8 - 0.6 * math.exp(-0.3 * layer)
    (daq, dak, dav, glq, glk, glv, glvt, glr, gla, xaq, gates) = _in_proj(
        x2, p["mix_norm_w"], p["w_in"], p["da_q_norm_w"], p["da_k_norm_w"], p["xa_q_norm_w"], p["gla_w_gate"],
        p["gla_b_gate"], seq=seq, tm=tm, interpret=interpret)
    oda = _diff_attn(daq, dak, dav, p["da_lambda"], p["da_out_norm_w"], batch=batch, seq=seq, blk_q=attn_blk_q,
                     blk_k=attn_blk_k, lam_init=lam_init, interpret=interpret)
    ogla = _gla(glq, glk, gla, glv, glvt, glr, p["gla_out_norm_w"], batch=batch, seq=seq, interpret=interpret)
    mk, mv = _mem_kv(mem2, p["mem_norm_w"], p["w_mem_kv"], p["xa_k_norm_w"], batch=batch, mem_len=mem_len,
                     interpret=interpret)
    x1, x8, scores = _merge(x2, oda, ogla, xaq, gates, mk, mv, p["w_br_da"], p["w_br_gla"], p["w_br_xa"], p["w_out"],
                             p["ffn_norm_w"], p["peer_w_q"], p["peer_sub_keys"], seq=seq, mem_len=mem_len, tm=tm,
                             interpret=interpret)
    idx_t, peer_gates = _peer_topk(scores, tb=peer_tb, interpret=interpret)
    w_exp = _peer_u(idx_t, x8, peer_gates, _pack_table(p["peer_u"]), tb=peer_tb, interpret=interpret)
    return _peer_v(idx_t, w_exp, x1, _pack_table(p["peer_v"]), tb=peer_tb, interpret=interpret)


_PARAM_NAMES = ("mix_norm_w", "w_in", "da_q_norm_w", "da_k_norm_w", "da_lambda", "da_out_norm_w", "gla_w_gate",
                "gla_b_gate", "gla_out_norm_w", "mem_norm_w", "w_mem_kv", "xa_q_norm_w", "xa_k_norm_w", "w_br_da",
                "w_br_gla", "w_br_xa", "w_out", "ffn_norm_w", "peer_w_q", "peer_sub_keys", "peer_u", "peer_v")


def _forward(x, mem, params, *, tm=256, attn_blk_q=512, attn_blk_k=1024, peer_tb=128, interpret=False):
    batch, seq, d = x.shape
    mem_len = mem.shape[1]
    x2 = x.reshape(batch * seq, d)
    mem2 = mem.reshape(batch * mem_len, d)
    depth = params["w_in"].shape[0]
    for layer in range(depth):
        p = {name: params[name][layer] for name in _PARAM_NAMES}
        x2 = _layer(x2, mem2, p, layer=layer, batch=batch, seq=seq, mem_len=mem_len, tm=tm, attn_blk_q=attn_blk_q,
                    attn_blk_k=attn_blk_k, peer_tb=peer_tb, interpret=interpret)
    return x2.reshape(batch, seq, d)


def kernel(x, mem, mix_norm_w, w_in, da_q_norm_w, da_k_norm_w, da_lambda, da_out_norm_w, gla_w_gate, gla_b_gate,
           gla_out_norm_w, mem_norm_w, w_mem_kv, xa_q_norm_w, xa_k_norm_w, w_br_da, w_br_gla, w_br_xa, w_out,
           ffn_norm_w, peer_w_q, peer_sub_keys, peer_u, peer_v):
    params = dict(zip(_PARAM_NAMES, (mix_norm_w, w_in, da_q_norm_w, da_k_norm_w, da_lambda, da_out_norm_w, gla_w_gate,
                                     gla_b_gate, gla_out_norm_w, mem_norm_w, w_mem_kv, xa_q_norm_w, xa_k_norm_w,
                                     w_br_da, w_br_gla, w_br_xa, w_out, ffn_norm_w, peer_w_q, peer_sub_keys, peer_u,
                                     peer_v)))
    return _forward(x, mem, params)
```

```python
import functools
import math

import jax
import jax.numpy as jnp
import numpy as np
from jax import lax
from jax.experimental import pallas as pl
from jax.experimental.pallas import tpu as pltpu

F32 = jnp.float32
BF16 = jnp.bfloat16

LANES = 128
SUBLANES = 8
VMEM_LIMIT_BYTES = 56 * 1024 * 1024

D_MODEL = 1024
DA_HEADS = 4
DA_HEAD_DIM = 64
DA_V_DIM = 2 * DA_HEAD_DIM
GLA_HEADS = 4
GLA_DK = 64
GLA_DV = 128
GLA_GATE_RANK = 16
GLA_TAU = 16.0
XA_HEADS = 4
XA_HEAD_DIM = 128
N_BRANCH = 3
PEER_HEADS = 8
PEER_N_KEYS = 128
PEER_QDIM = 256
PEER_TOPK = 16
RMS_EPS = 1e-6
LOG2E = math.log2(math.e)

DA_W = DA_HEADS * 2 * DA_HEAD_DIM
GLA_QW = GLA_HEADS * GLA_DK
GLA_VW = GLA_HEADS * GLA_DV
XA_W = XA_HEADS * XA_HEAD_DIM
GATE_W = N_BRANCH * D_MODEL
PEER_SLOTS = PEER_HEADS * PEER_TOPK

C_DAQ = 0
C_DAK = C_DAQ + DA_W
C_DAV = C_DAK + DA_W
C_GLQ = C_DAV + DA_W
C_GLK = C_GLQ + GLA_QW
C_GLV = C_GLK + GLA_QW
C_GLR = C_GLV + GLA_VW
C_XAQ = C_GLR + GLA_VW
C_GATE = C_XAQ + XA_W
C_GLG = C_GATE + GATE_W
C_END = C_GLG + LANES

GLA_CHUNK = 128
GLA_LEVELS = 7
GLA_CHUNKS_PER_STEP = 2

ROW_SUBLANES = 4
ROW_PIECES = 2 * ROW_SUBLANES
PEER_TOKENS_PER_STEP = 64


def _cparams(sem):
    return pltpu.CompilerParams(dimension_semantics=sem, vmem_limit_bytes=VMEM_LIMIT_BYTES)


def _resident(shape):
    nd = len(shape)
    return pl.BlockSpec(shape, lambda *_: (0,) * nd, pipeline_mode=pl.Buffered(1))


def _dot(a, b):
    return jnp.dot(a, b, preferred_element_type=F32)


def _dot_nt(a, b):
    return lax.dot_general(a, b, (((1,), (1,)), ((), ())), preferred_element_type=F32)


def _split2(t):
    hi = t.astype(BF16)
    lo = (t - hi.astype(F32)).astype(BF16)
    return hi, lo


def _group_sumsq(t, bd_ref):
    hi, lo = _split2(t * t)
    bd = bd_ref[...]
    return _dot(hi, bd) + _dot(lo, bd)


def _in_proj_body(x_ref, nw_ref, w_ref, wvt_ref, qw_ref, kw_ref, xw_ref, wg_ref, bg_ref, bd64_ref, bd128_ref,
                  daq_ref, dak_ref, dav_ref, glq_ref, glk_ref, glv_ref, glvt_ref, glr_ref, gla_ref, xaq_ref,
                  gate_ref, *, seq):
    x = x_ref[...]
    ms = jnp.mean(x * x, axis=-1, keepdims=True)
    xn = (x * lax.rsqrt(ms + RMS_EPS) * nw_ref[...]).astype(BF16)

    def proj(c0, width):
        return _dot(xn, w_ref[:, c0:c0 + width])

    q = proj(C_DAQ, DA_W)
    q = q * lax.rsqrt(_group_sumsq(q, bd64_ref) * (1.0 / DA_HEAD_DIM) + RMS_EPS) * qw_ref[...]
    k = proj(C_DAK, DA_W)
    k = k * lax.rsqrt(_group_sumsq(k, bd64_ref) * (1.0 / DA_HEAD_DIM) + RMS_EPS) * kw_ref[...]
    tm = x.shape[0]
    lane = lax.broadcasted_iota(jnp.int32, (1, LANES), 1)
    pos = (pl.program_id(0) * tm + lax.broadcasted_iota(jnp.int32, (tm, 1), 0)) % seq
    pos = pos.astype(F32)
    for h in range(DA_HEADS):
        hs = slice(h * DA_V_DIM, (h + 1) * DA_V_DIM)
        b0 = pos * (LOG2E * 2.0 ** (-8.0 * (h + 1) / DA_HEADS))
        p0 = b0.astype(BF16).astype(F32)
        b1 = b0 - p0
        p1 = b1.astype(BF16).astype(F32)
        p2 = b1 - p1
        for c in range(2):
            own = (lane < DA_HEAD_DIM) if c == 0 else (lane >= DA_HEAD_DIM)
            first = DA_HEAD_DIM * (1 - c)
            cs = slice((2 * h + c) * LANES, (2 * h + c + 1) * LANES)
            ones = ((lane >= first) & (lane < first + 3)).astype(F32)
            daq_ref[:, cs] = jnp.where(own, q[:, hs], ones).astype(BF16)
            feat = jnp.where(lane == first, p0, jnp.where(lane == first + 1, p1, jnp.where(lane == first + 2, p2, 0.0)))
            dak_ref[:, cs] = jnp.where(own, k[:, hs], feat).astype(BF16)
    dav_ref[...] = proj(C_DAV, DA_W).astype(BF16)

    glq_ref[...] = proj(C_GLQ, GLA_QW) * (GLA_DK ** -0.5)
    glk_ref[...] = proj(C_GLK, GLA_QW)
    glv_ref[...] = proj(C_GLV, GLA_VW).astype(BF16)
    glvt_ref[...] = _dot_nt(wvt_ref[...], xn).astype(BF16)
    r = proj(C_GLR, GLA_VW)
    glr_ref[...] = (r * jax.nn.sigmoid(r)).astype(BF16)

    g = proj(C_GLG, LANES)
    z = jnp.dot(g, wg_ref[...], preferred_element_type=F32, precision=lax.Precision.HIGHEST) + bg_ref[...]
    log_sig = jnp.minimum(z, 0.0) - jnp.log1p(jnp.exp(-jnp.abs(z)))
    gla_ref[...] = log_sig * (1.0 / GLA_TAU)

    xq = proj(C_XAQ, XA_W)
    xaq_ref[...] = (xq * lax.rsqrt(_group_sumsq(xq, bd128_ref) * (1.0 / XA_HEAD_DIM) + RMS_EPS) * xw_ref[...]).astype(BF16)

    for c in range(0, GATE_W, 512):
        gate_ref[:, c:c + 512] = jax.nn.sigmoid(proj(C_GATE + c, 512)).astype(BF16)


def _block_diag_ones(width, group):
    idx = np.arange(width) // group
    return jnp.asarray((idx[:, None] == idx[None, :]).astype(np.float32), dtype=BF16)


def _in_proj(x2, mix_norm_w, w_in, da_q_norm_w, da_k_norm_w, xa_q_norm_w, gla_w_gate, gla_b_gate, *, seq, tm,
             interpret):
    n = x2.shape[0]
    c_glg_src = C_GLR + GLA_VW
    w = jnp.concatenate(
        [w_in[:, :c_glg_src], w_in[:, c_glg_src + GLA_GATE_RANK:], w_in[:, c_glg_src:c_glg_src + GLA_GATE_RANK],
         jnp.zeros((D_MODEL, LANES - GLA_GATE_RANK), w_in.dtype)], axis=1).astype(BF16)
    wvt = w_in[:, C_GLV:C_GLV + GLA_VW].T.astype(BF16)
    qw = jnp.tile(da_q_norm_w, 2 * DA_HEADS)[None, :] * (DA_HEAD_DIM ** -0.5 * LOG2E)
    kw = jnp.tile(da_k_norm_w, 2 * DA_HEADS)[None, :]
    xw = jnp.tile(xa_q_norm_w, XA_HEADS)[None, :] * (XA_HEAD_DIM ** -0.5)
    wg = jnp.zeros((LANES, GLA_QW), F32).at[:GLA_GATE_RANK].set(gla_w_gate)
    bg = gla_b_gate[None, :]
    row = lambda width: pl.BlockSpec((tm, width), lambda i: (i, 0))
    out_widths = [(2 * DA_W, BF16), (2 * DA_W, BF16), (DA_W, BF16), (GLA_QW, F32), (GLA_QW, F32), (GLA_VW, BF16)]
    out_shape = [jax.ShapeDtypeStruct((n, wd), dt) for wd, dt in out_widths]
    out_specs = [row(wd) for wd, _ in out_widths]
    out_shape.append(jax.ShapeDtypeStruct((GLA_VW, n), BF16))
    out_specs.append(pl.BlockSpec((GLA_VW, tm), lambda i: (0, i)))
    for wd, dt in [(GLA_VW, BF16), (GLA_QW, F32), (XA_W, BF16), (GATE_W, BF16)]:
        out_shape.append(jax.ShapeDtypeStruct((n, wd), dt))
        out_specs.append(row(wd))
    return pl.pallas_call(
        functools.partial(_in_proj_body, seq=seq),
        grid=(n // tm,),
        in_specs=[row(D_MODEL), _resident((1, D_MODEL)), _resident((D_MODEL, C_END)), _resident((GLA_VW, D_MODEL)),
                  _resident((1, DA_W)), _resident((1, DA_W)), _resident((1, XA_W)), _resident((LANES, GLA_QW)),
                  _resident((1, GLA_QW)), _resident((DA_W, DA_W)), _resident((XA_W, XA_W))],
        out_specs=out_specs,
        out_shape=out_shape,
        compiler_params=_cparams(("parallel",)),
        name="in_proj",
        interpret=interpret,
    )(x2, mix_norm_w[None, :], w, wvt, qw, kw, xw, wg, bg, _block_diag_ones(DA_W, DA_HEAD_DIM),
      _block_diag_ones(XA_W, XA_HEAD_DIM))


def _diff_attn_body(qi_tab, kj_tab, q_ref, k_ref, v_ref, lam_ref, ow_ref, o_ref, m_scr, acc_scr, *, blk_q, blk_k,
                    lam_init):
    p = pl.program_id(1)
    qi = qi_tab[p]
    kj = kj_tab[p]
    last = (qi * blk_q) // blk_k

    @pl.when(kj == 0)
    def _():
        m_scr[...] = jnp.full(m_scr.shape, -jnp.inf, F32)
        acc_scr[...] = jnp.zeros(acc_scr.shape, F32)

    def component(h, c, causal):
        v = v_ref[:, h * DA_V_DIM:(h + 1) * DA_V_DIM]
        v_ones = jnp.concatenate([v, jnp.ones_like(v)], axis=1)
        cs = slice((2 * h + c) * LANES, (2 * h + c + 1) * LANES)
        s = _dot_nt(q_ref[:, cs], k_ref[:, cs])
        if causal:
            row = lax.broadcasted_iota(jnp.int32, (blk_q, 1), 0) + qi * blk_q
            col = lax.broadcasted_iota(jnp.int32, (1, blk_k), 1) + kj * blk_k
            s = jnp.where(col <= row, s, -jnp.inf)
        i = 2 * h + c
        m_prev = m_scr[i]
        m_new = jnp.maximum(m_prev, jnp.max(s, axis=-1, keepdims=True))
        acc_scr[i] = jnp.exp2(m_prev - m_new) * acc_scr[i] + _dot(jnp.exp2(s - m_new).astype(BF16), v_ones)
        m_scr[i] = m_new

    @pl.when(kj < last)
    def _():
        for h in range(DA_HEADS):
            component(h, 0, False)
            component(h, 1, False)

    @pl.when(kj == last)
    def _():
        lv = lam_ref[...]
        lam = (jnp.exp(jnp.sum(lv[0:1] * lv[1:2], axis=-1, keepdims=True))
               - jnp.exp(jnp.sum(lv[2:3] * lv[3:4], axis=-1, keepdims=True)) + lam_init)
        for h in range(DA_HEADS):
            component(h, 0, True)
            component(h, 1, True)
            a0 = acc_scr[2 * h]
            a1 = acc_scr[2 * h + 1]
            o = a0[:, :DA_V_DIM] / a0[:, DA_V_DIM:] - lam * (a1[:, :DA_V_DIM] / a1[:, DA_V_DIM:])
            ms = jnp.mean(o * o, axis=-1, keepdims=True)
            o_ref[:, h * DA_V_DIM:(h + 1) * DA_V_DIM] = (
                o * lax.rsqrt(ms + RMS_EPS) * ow_ref[...] * (1.0 - lam_init)).astype(BF16)


def _diff_attn(daq, dak, dav, da_lambda, da_out_norm_w, *, batch, seq, blk_q, blk_k, lam_init, interpret):
    assert blk_k % blk_q == 0 and seq % blk_k == 0
    nq = seq // blk_q
    nk = seq // blk_k
    pairs = [(qi, kj) for qi in range(nq) for kj in range((qi * blk_q) // blk_k + 1)]
    qi_tab = jnp.asarray([p[0] for p in pairs], jnp.int32)
    kj_tab = jnp.asarray([p[1] for p in pairs], jnp.int32)
    qmap = lambda b, p, qt, kt: (b * nq + qt[p], 0)
    kmap = lambda b, p, qt, kt: (b * nk + kt[p], 0)
    n_comp = 2 * DA_HEADS
    grid_spec = pltpu.PrefetchScalarGridSpec(
        num_scalar_prefetch=2,
        grid=(batch, len(pairs)),
        in_specs=[pl.BlockSpec((blk_q, n_comp * LANES), qmap), pl.BlockSpec((blk_k, n_comp * LANES), kmap),
                  pl.BlockSpec((blk_k, DA_HEADS * DA_V_DIM), kmap),
                  pl.BlockSpec((4, DA_HEAD_DIM), lambda b, p, qt, kt: (0, 0)),
                  pl.BlockSpec((1, DA_V_DIM), lambda b, p, qt, kt: (0, 0))],
        out_specs=pl.BlockSpec((blk_q, DA_HEADS * DA_V_DIM), qmap),
        scratch_shapes=[pltpu.VMEM((n_comp, blk_q, 1), F32), pltpu.VMEM((n_comp, blk_q, 2 * DA_V_DIM), F32)],
    )
    return pl.pallas_call(
        functools.partial(_diff_attn_body, blk_q=blk_q, blk_k=blk_k, lam_init=lam_init),
        grid_spec=grid_spec,
        out_shape=jax.ShapeDtypeStruct((batch * seq, DA_HEADS * DA_V_DIM), BF16),
        compiler_params=_cparams(("parallel", "arbitrary")),
        name="diff_attn",
        interpret=interpret,
    )(qi_tab, kj_tab, daq, dak, dav, da_lambda, da_out_norm_w[None, :])


def _gla_constants():
    c = GLA_CHUNK
    t = np.arange(c)[:, None]
    u = np.arange(c)[None, :]
    lmats, masks = [], []
    for lev in range(GLA_LEVELS):
        m = 1 << lev
        second = (t % (2 * m)) >= m
        first = ~second
        bnd = (t // m) * m
        lmats.append(second & (u >= bnd) & (u <= t))
    for lev in range(GLA_LEVELS):
        m = 1 << lev
        first = (t % (2 * m)) < m
        end = (t // m) * m + m - 1
        lmats.append(first & (u > t) & (u <= end))
    lmats.append(u <= t)
    lmats.append(u > t)
    for lev in range(GLA_LEVELS):
        m = 1 << lev
        s = np.arange(c)[None, :]
        masks.append(((t // (2 * m)) == (s // (2 * m))) & ((t % (2 * m)) >= m) & ((s % (2 * m)) < m))
    masks.append(t == np.arange(c)[None, :])
    lall = jnp.asarray(np.concatenate(lmats, axis=0).astype(np.float32), dtype=BF16)
    mall = jnp.asarray(np.stack(masks, axis=0).astype(np.float32))
    return lall, mall


def _gla_body(q_ref, k_ref, la_ref, v_ref, vt_ref, r_ref, lall_ref, mall_ref, ow_ref, o_ref, state_scr):
    c = GLA_CHUNK

    @pl.when(pl.program_id(1) == 0)
    def _():
        state_scr[...] = jnp.zeros(state_scr.shape, F32)

    lall = lall_ref[...]
    lane = lax.broadcasted_iota(jnp.int32, (1, GLA_QW), 1)
    head_masks = [(lane // GLA_DK == h).astype(F32) for h in range(GLA_HEADS)]

    def intra(rs):
        q = q_ref[rs, :]
        k = k_ref[rs, :]
        g = la_ref[rs, :]
        g1 = g.astype(BF16)
        r1 = g - g1.astype(F32)
        g2 = r1.astype(BF16)
        g3 = (r1 - g2.astype(F32)).astype(BF16)
        e_all = jnp.exp(_dot(lall, g1) + _dot(lall, g2) + _dot(lall, g3))

        def rows(i):
            return e_all[i * c:(i + 1) * c]

        a = [jnp.zeros((c, c), F32) for _ in range(GLA_HEADS)]
        for lev in range(GLA_LEVELS + 1):
            if lev < GLA_LEVELS:
                ql = q * rows(lev)
                kl = (k * rows(GLA_LEVELS + lev)).astype(BF16)
            else:
                ql = q
                kl = k.astype(BF16)
            mask = mall_ref[lev]
            for h in range(GLA_HEADS):
                a[h] = a[h] + mask * _dot_nt((ql * head_masks[h]).astype(BF16), kl)
        e_b = rows(2 * GLA_LEVELS)
        return a, q * e_b, (k * rows(2 * GLA_LEVELS + 1)).astype(BF16), e_b[c - 1:c, :]

    chunks = [slice(i * c, (i + 1) * c) for i in range(GLA_CHUNKS_PER_STEP)]
    parts = [intra(rs) for rs in chunks]
    for rs, (a, q_dec, k_dec, chunk_decay) in zip(chunks, parts):
        for h in range(GLA_HEADS):
            vs = slice(h * GLA_DV, (h + 1) * GLA_DV)
            st = state_scr[h]
            o = (_dot(a[h].astype(BF16), v_ref[rs, vs])
                 + _dot_nt((q_dec * head_masks[h]).astype(BF16), st.astype(BF16)))
            state_scr[h] = st * chunk_decay + _dot(vt_ref[vs, rs], k_dec)
            ms = jnp.mean(o * o, axis=-1, keepdims=True)
            o_ref[rs, vs] = (o * lax.rsqrt(ms + RMS_EPS) * ow_ref[...] * r_ref[rs, vs].astype(F32)).astype(BF16)


def _gla(glq, glk, gla, glv, glvt, glr, gla_out_norm_w, *, batch, seq, interpret):
    rows_per_step = GLA_CHUNK * GLA_CHUNKS_PER_STEP
    ns = seq // rows_per_step
    lall, mall = _gla_constants()
    row = lambda width: pl.BlockSpec((rows_per_step, width), lambda b, i: (b * ns + i, 0))
    return pl.pallas_call(
        _gla_body,
        grid=(batch, ns),
        in_specs=[row(GLA_QW), row(GLA_QW), row(GLA_QW), row(GLA_VW),
                  pl.BlockSpec((GLA_VW, rows_per_step), lambda b, i: (0, b * ns + i)), row(GLA_VW),
                  _resident(lall.shape), _resident(mall.shape), _resident((1, GLA_DV))],
        out_specs=row(GLA_VW),
        out_shape=jax.ShapeDtypeStruct((batch * seq, GLA_VW), BF16),
        scratch_shapes=[pltpu.VMEM((GLA_HEADS, GLA_DV, GLA_QW), F32)],
        compiler_params=_cparams(("parallel", "arbitrary")),
        name="gla",
        interpret=interpret,
    )(glq, glk, gla, glv, glvt, glr, lall, mall, gla_out_norm_w[None, :])


def _mem_kv_body(mem_ref, nw_ref, w_ref, kw_ref, bd_ref, mk_ref, mv_ref):
    x = mem_ref[...]
    ms = jnp.mean(x * x, axis=-1, keepdims=True)
    xn = (x * lax.rsqrt(ms + RMS_EPS) * nw_ref[...]).astype(BF16)
    k = _dot(xn, w_ref[:, :XA_W])
    mk_ref[...] = (k * lax.rsqrt(_group_sumsq(k, bd_ref) * (1.0 / XA_HEAD_DIM) + RMS_EPS) * kw_ref[...]).astype(BF16)
    mv_ref[...] = _dot(xn, w_ref[:, XA_W:]).astype(BF16)


def _mem_kv(mem2, mem_norm_w, w_mem_kv, xa_k_norm_w, *, batch, mem_len, interpret):
    blk = pl.BlockSpec((mem_len, XA_W), lambda b: (b, 0))
    return pl.pallas_call(
        _mem_kv_body,
        grid=(batch,),
        in_specs=[pl.BlockSpec((mem_len, D_MODEL), lambda b: (b, 0)), _resident((1, D_MODEL)),
                  _resident((D_MODEL, 2 * XA_W)), _resident((1, XA_W)), _resident((XA_W, XA_W))],
        out_specs=[blk, blk],
        out_shape=[jax.ShapeDtypeStruct((batch * mem_len, XA_W), BF16)] * 2,
        compiler_params=_cparams(("parallel",)),
        name="mem_kv",
        interpret=interpret,
    )(mem2, mem_norm_w[None, :], w_mem_kv.astype(BF16), jnp.tile(xa_k_norm_w, XA_HEADS)[None, :],
      _block_diag_ones(XA_W, XA_HEAD_DIM))


def _piece_offset(j):
    return (j % 2) * (D_MODEL // 2) + (j // 2) * LANES


def _merge_body(x_ref, oda_ref, ogla_ref, xaq_ref, gate_ref, mk_ref, mv_ref, wda_ref, wgla_ref, wxa_ref, wout_ref,
                fw_ref, wq_ref, keys_ref, pair_code_ref, x1_ref, x8_ref, idx_ref, g_ref):
    tm = x_ref.shape[0]
    br_xa = None
    for h in range(XA_HEADS):
        hs = slice(h * XA_HEAD_DIM, (h + 1) * XA_HEAD_DIM)
        s = _dot_nt(xaq_ref[:, hs], mk_ref[:, hs])
        s = s - jnp.max(s, axis=-1, keepdims=True)
        p = jnp.exp(s)
        p = p / jnp.sum(p, axis=-1, keepdims=True)
        o = _dot(p.astype(BF16), mv_ref[:, hs])
        t = _dot(o.astype(BF16), wxa_ref[hs, :])
        br_xa = t if br_xa is None else br_xa + t
    merged = (gate_ref[:, 0:D_MODEL].astype(F32) * _dot(oda_ref[...], wda_ref[...])
              + gate_ref[:, D_MODEL:2 * D_MODEL].astype(F32) * _dot(ogla_ref[...], wgla_ref[...])
              + gate_ref[:, 2 * D_MODEL:].astype(F32) * br_xa)
    x1 = x_ref[...] + _dot(merged.astype(BF16), wout_ref[...])
    x1_ref[...] = x1
    ms = jnp.mean(x1 * x1, axis=-1, keepdims=True)
    xn = x1 * lax.rsqrt(ms + RMS_EPS) * fw_ref[...]
    xnb = xn.astype(BF16)
    for j in range(ROW_PIECES):
        off = _piece_offset(j)
        x8_ref[pl.ds(j, tm, stride=ROW_PIECES), :] = xn[:, off:off + LANES]
    pq = _dot(xnb, wq_ref[...])
    half = PEER_QDIM // 2

    def score(hp):
        q_hi, q_lo = _split2(pq[:, hp * half:(hp + 1) * half])
        keys = keys_ref[hp % 2]
        return _dot_nt(keys, q_hi) + _dot_nt(keys, q_lo)

    _topk_slots(score, pair_code_ref, idx_ref, g_ref)


def _merge(x2, oda, ogla, xaq, gates, mk, mv, w_br_da, w_br_gla, w_br_xa, w_out, ffn_norm_w, peer_w_q, peer_sub_keys,
           *, seq, mem_len, tm, interpret):
    n = x2.shape[0]
    spb = seq // tm
    row = lambda width: pl.BlockSpec((tm, width), lambda i: (i, 0))
    mem_blk = pl.BlockSpec((mem_len, XA_W), lambda i: (i // spb, 0))
    pair_code = jnp.asarray(_pair_candidates()[1])
    return pl.pallas_call(
        _merge_body,
        grid=(n // tm,),
        in_specs=[row(D_MODEL), row(DA_W), row(GLA_VW), row(XA_W), row(GATE_W), mem_blk, mem_blk,
                  _resident((DA_W, D_MODEL)), _resident((GLA_VW, D_MODEL)), _resident((XA_W, D_MODEL)),
                  _resident((D_MODEL, D_MODEL)), _resident((1, D_MODEL)),
                  _resident((D_MODEL, PEER_HEADS * PEER_QDIM)), _resident((2, PEER_N_KEYS, PEER_QDIM // 2)),
                  _resident(pair_code.shape)],
        out_specs=[row(D_MODEL), pl.BlockSpec((tm * ROW_PIECES, LANES), lambda i: (i, 0)),
                   pl.BlockSpec((PEER_SLOTS, tm), lambda i: (0, i)), row(PEER_SLOTS)],
        out_shape=[jax.ShapeDtypeStruct((n, D_MODEL), F32), jax.ShapeDtypeStruct((n * ROW_PIECES, LANES), F32),
                   jax.ShapeDtypeStruct((PEER_SLOTS, n), jnp.int32), jax.ShapeDtypeStruct((n, PEER_SLOTS), F32)],
        compiler_params=_cparams(("parallel",)),
        name="merge",
        interpret=interpret,
    )(x2, oda, ogla, xaq, gates, mk, mv, w_br_da.astype(BF16), w_br_gla.astype(BF16), w_br_xa.astype(BF16),
      w_out.astype(BF16), ffn_norm_w[None, :], peer_w_q.astype(BF16), peer_sub_keys.astype(BF16), pair_code)


def _top16(x, code):
    vals, codes = [], []
    for _ in range(PEER_TOPK):
        m = jnp.max(x, axis=0, keepdims=True)
        win = jnp.min(jnp.where(x == m, code, jnp.inf), axis=0, keepdims=True)
        vals.append(m)
        codes.append(win)
        x = jnp.where(code == win, -jnp.inf, x)
    return jnp.concatenate(vals, axis=0), jnp.concatenate(codes, axis=0).astype(jnp.int32)


def _select_rows(table, pos):
    out = jnp.zeros(pos.shape, table.dtype)
    for i in range(PEER_TOPK):
        out = jnp.where(pos == i, table[i:i + 1, :], out)
    return out


def _pair_candidates():
    blocks = [(0, 1, 0, PEER_TOPK)] + [(i, i + 1, 0, SUBLANES) for i in range(1, SUBLANES)]
    blocks.append((SUBLANES, PEER_TOPK, 0, 1))
    codes = np.concatenate([np.array([i * PEER_TOPK + j for i in range(i0, i1) for j in range(j0, j1)])
                            for i0, i1, j0, j1 in blocks])
    return blocks, codes.astype(np.float32)[:, None]


def _topk_slots(score, pair_code_ref, idx_ref, g_ref):
    blocks, _ = _pair_candidates()
    tb = idx_ref.shape[1]
    key_code = lax.broadcasted_iota(jnp.int32, (PEER_N_KEYS, tb), 0).astype(F32)
    pair_code = jnp.broadcast_to(pair_code_ref[...], (pair_code_ref.shape[0], tb))
    experts, gates = [], []
    for h in range(PEER_HEADS):
        s0, i0 = _top16(score(2 * h), key_code)
        s1, i1 = _top16(score(2 * h + 1), key_code)
        cand = jnp.concatenate([s0[a0:a1, :] + s1[b0:b1, :] for a0, a1, b0, b1 in blocks], axis=0)
        best, flat = _top16(cand, pair_code)
        expert = (_select_rows(i0, jnp.right_shift(flat, 4)) * PEER_N_KEYS
                  + _select_rows(i1, jnp.bitwise_and(flat, PEER_TOPK - 1)))
        e = jnp.exp(best - best[0:1, :])
        odd = jnp.bitwise_and(lax.broadcasted_iota(jnp.int32, expert.shape, 0), 1)
        experts.append(expert * ROW_SUBLANES + (SUBLANES - ROW_SUBLANES * odd))
        gates.append(e / jnp.sum(e, axis=0, keepdims=True))
    idx_ref[...] = jnp.concatenate(experts, axis=0)
    g_ref[...] = jnp.concatenate(gates, axis=0).T


def _pack_table(t):
    e = t.shape[0]
    bits = lax.bitcast_convert_type(t.astype(BF16), jnp.uint16).astype(jnp.uint32)
    bits = bits.reshape(e, 2, ROW_SUBLANES, LANES)
    packed = (bits[:, 0] | (bits[:, 1] << 16)).reshape(e * ROW_SUBLANES, LANES)
    pad = jnp.zeros((SUBLANES, LANES), jnp.uint32)
    return jnp.concatenate([pad, packed, pad], axis=0)


def _expert_rows(idx_scr, tab_ref, t):
    low = lax.broadcasted_iota(jnp.int32, (SUBLANES, LANES), 0) < ROW_SUBLANES
    tiles = [tab_ref[pl.ds(pl.multiple_of(idx_scr.at[k][t], ROW_SUBLANES), SUBLANES), :] for k in range(PEER_SLOTS)]
    pairs = [jnp.where(low, tiles[2 * m], tiles[2 * m + 1]) for m in range(PEER_SLOTS // 2)]
    return pltpu.bitcast(jnp.concatenate(pairs, axis=0), BF16)


def _token_blocks(idx_hbm, idx_a, idx_b, sem, per_token, tb):
    i = pl.program_id(0)

    def copy(blk, scr, s):
        return pltpu.make_async_copy(idx_hbm.at[:, pl.ds(pl.multiple_of(blk * tb, LANES), tb)], scr, sem.at[s])

    def run_block(idx_scr, base):
        def body(j, carry):
            for u in range(PEER_TOKENS_PER_STEP):
                per_token(idx_scr, PEER_TOKENS_PER_STEP * j + u, base)
            return carry

        lax.fori_loop(0, tb // PEER_TOKENS_PER_STEP, body, 0)

    @pl.when(i == 0)
    def _():
        copy(0, idx_a, 0).start()

    copy(2 * i, idx_a, 0).wait()
    copy(2 * i + 1, idx_b, 1).start()
    run_block(idx_a, 0)
    copy(2 * i + 1, idx_b, 1).wait()

    @pl.when(i + 1 < pl.num_programs(0))
    def _():
        copy(2 * i + 2, idx_a, 0).start()

    run_block(idx_b, tb)


def _piece_mask():
    r = lax.broadcasted_iota(jnp.int32, (ROW_PIECES, PEER_SLOTS * ROW_PIECES), 0)
    c = lax.broadcasted_iota(jnp.int32, (ROW_PIECES, PEER_SLOTS * ROW_PIECES), 1)
    return (c % ROW_PIECES == r).astype(F32)


def _peer_u_body(idx_hbm, x8_ref, g_ref, tab_ref, sel_ref, selt_ref, w_ref, idx_a, idx_b, sem, r_scr, *, tb):
    dmask = _piece_mask()

    def per_token(idx_scr, t, base):
        x8 = x8_ref[pl.ds(pl.multiple_of((base + t) * ROW_PIECES, ROW_PIECES), ROW_PIECES), :].astype(BF16)
        r = _dot_nt(x8, _expert_rows(idx_scr, tab_ref, t))
        r_scr[pl.ds(base + t, 1), :] = jnp.sum(r * dmask, axis=0, keepdims=True)

    _token_blocks(idx_hbm, idx_a, idx_b, sem, per_token, tb)
    hi, lo = _split2(r_scr[...])
    s = _dot(hi, sel_ref[...]) + _dot(lo, sel_ref[...])
    act = 0.5 * s * (1.0 + lax.erf(s * (2.0 ** -0.5)))
    w = (g_ref[...] * act).astype(BF16)
    w_ref[...] = _dot(w, selt_ref[...])


def _peer_v_body(idx_hbm, w_ref, x1_ref, tab_ref, o_ref, idx_a, idx_b, sem, o8_scr, *, tb):
    dmask = _piece_mask()

    def per_token(idx_scr, t, base):
        lhs = (w_ref[pl.ds(base + t, 1), :] * dmask).astype(BF16)
        o8_scr[pl.ds(pl.multiple_of((base + t) * ROW_PIECES, ROW_PIECES), ROW_PIECES), :] = _dot(
            lhs, _expert_rows(idx_scr, tab_ref, t))

    _token_blocks(idx_hbm, idx_a, idx_b, sem, per_token, tb)
    for j in range(ROW_PIECES):
        cs = slice(_piece_offset(j), _piece_offset(j) + LANES)
        o_ref[:, cs] = x1_ref[:, cs] + o8_scr[pl.ds(j, 2 * tb, stride=ROW_PIECES), :]


def _idx_scratch(tb):
    return [pltpu.SMEM((PEER_SLOTS, tb), jnp.int32), pltpu.SMEM((PEER_SLOTS, tb), jnp.int32),
            pltpu.SemaphoreType.DMA((2,))]


def _peer_u(idx_t, x8, gates, table, *, tb, interpret):
    n = idx_t.shape[1]
    ts = 2 * tb
    sel_np = (np.arange(PEER_SLOTS * ROW_PIECES)[:, None] // ROW_PIECES == np.arange(PEER_SLOTS)[None, :])
    sel = jnp.asarray(sel_np.astype(np.float32), dtype=BF16)
    return pl.pallas_call(
        functools.partial(_peer_u_body, tb=tb),
        grid=(n // ts,),
        in_specs=[pl.BlockSpec(memory_space=pl.ANY),
                  pl.BlockSpec((ts * ROW_PIECES, LANES), lambda i: (i, 0)),
                  pl.BlockSpec((ts, PEER_SLOTS), lambda i: (i, 0)),
                  _resident(table.shape), _resident(sel.shape), _resident(sel.T.shape)],
        out_specs=pl.BlockSpec((ts, PEER_SLOTS * ROW_PIECES), lambda i: (i, 0)),
        out_shape=jax.ShapeDtypeStruct((n, PEER_SLOTS * ROW_PIECES), F32),
        scratch_shapes=_idx_scratch(tb) + [pltpu.VMEM((ts, PEER_SLOTS * ROW_PIECES), F32)],
        compiler_params=_cparams(("arbitrary",)),
        name="peer_u",
        interpret=interpret,
    )(idx_t, x8, gates, table, sel, sel.T)


def _peer_v(idx_t, w_exp, x1, table, *, tb, interpret):
    n = idx_t.shape[1]
    ts = 2 * tb
    return pl.pallas_call(
        functools.partial(_peer_v_body, tb=tb),
        grid=(n // ts,),
        in_specs=[pl.BlockSpec(memory_space=pl.ANY),
                  pl.BlockSpec((ts, PEER_SLOTS * ROW_PIECES), lambda i: (i, 0)),
                  pl.BlockSpec((ts, D_MODEL), lambda i: (i, 0)),
                  _resident(table.shape)],
        out_specs=pl.BlockSpec((ts, D_MODEL), lambda i: (i, 0)),
        out_shape=jax.ShapeDtypeStruct((n, D_MODEL), F32),
        scratch_shapes=_idx_scratch(tb) + [pltpu.VMEM((ts * ROW_PIECES, LANES), F32)],
        compiler_params=_cparams(("arbitrary",)),
        name="peer_v",
        interpret=interpret,
    )(idx_t, w_exp, x1, table)


def _layer(x2, mem2, p, *, layer, batch, seq, mem_len, tm, attn_blk_q, attn_blk_k, peer_tb, interpret):
    lam_init = 0.8 - 0.6 * math.exp(-0.3 * layer)
    (daq, dak, dav, glq, glk, glv, glvt, glr, gla, xaq, gates) = _in_proj(
        x2, p["mix_norm_w"], p["w_in"], p["da_q_norm_w"], p["da_k_norm_w"], p["xa_q_norm_w"], p["gla_w_gate"],
        p["gla_b_gate"], seq=seq, tm=tm, interpret=interpret)
    oda = _diff_attn(daq, dak, dav, p["da_lambda"], p["da_out_norm_w"], batch=batch, seq=seq, blk_q=attn_blk_q,
                     blk_k=attn_blk_k, lam_init=lam_init, interpret=interpret)
    ogla = _gla(glq, glk, gla, glv, glvt, glr, p["gla_out_norm_w"], batch=batch, seq=seq, interpret=interpret)
    mk, mv = _mem_kv(mem2, p["mem_norm_w"], p["w_mem_kv"], p["xa_k_norm_w"], batch=batch, mem_len=mem_len,
                     interpret=interpret)
    x1, x8, idx_t, peer_gates = _merge(x2, oda, ogla, xaq, gates, mk, mv, p["w_br_da"], p["w_br_gla"], p["w_br_xa"], p["w_out"],
                             p["ffn_norm_w"], p["peer_w_q"], p["peer_sub_keys"], seq=seq, mem_len=mem_len, tm=tm,
                             interpret=interpret)
    w_exp = _peer_u(idx_t, x8, peer_gates, _pack_table(p["peer_u"]), tb=peer_tb, interpret=interpret)
    return _peer_v(idx_t, w_exp, x1, _pack_table(p["peer_v"]), tb=peer_tb, interpret=interpret)


_PARAM_NAMES = ("mix_norm_w", "w_in", "da_q_norm_w", "da_k_norm_w", "da_lambda", "da_out_norm_w", "gla_w_gate",
                "gla_b_gate", "gla_out_norm_w", "mem_norm_w", "w_mem_kv", "xa_q_norm_w", "xa_k_norm_w", "w_br_da",
                "w_br_gla", "w_br_xa", "w_out", "ffn_norm_w", "peer_w_q", "peer_sub_keys", "peer_u", "peer_v")


def _forward(x, mem, params, *, tm=256, attn_blk_q=512, attn_blk_k=1024, peer_tb=128, interpret=False):
    batch, seq, d = x.shape
    mem_len = mem.shape[1]
    x2 = x.reshape(batch * seq, d)
    mem2 = mem.reshape(batch * mem_len, d)
    depth = params["w_in"].shape[0]
    for layer in range(depth):
        p = {name: params[name][layer] for name in _PARAM_NAMES}
        x2 = _layer(x2, mem2, p, layer=layer, batch=batch, seq=seq, mem_len=mem_len, tm=tm, attn_blk_q=attn_blk_q,
                    attn_blk_k=attn_blk_k, peer_tb=peer_tb, interpret=interpret)
    return x2.reshape(batch, seq, d)


def kernel(x, mem, mix_norm_w, w_in, da_q_norm_w, da_k_norm_w, da_lambda, da_out_norm_w, gla_w_gate, gla_b_gate,
           gla_out_norm_w, mem_norm_w, w_mem_kv, xa_q_norm_w, xa_k_norm_w, w_br_da, w_br_gla, w_br_xa, w_out,
           ffn_norm_w, peer_w_q, peer_sub_keys, peer_u, peer_v):
    params = dict(zip(_PARAM_NAMES, (mix_norm_w, w_in, da_q_norm_w, da_k_norm_w, da_lambda, da_out_norm_w, gla_w_gate,
                                     gla_b_gate, gla_out_norm_w, mem_norm_w, w_mem_kv, xa_q_norm_w, xa_k_norm_w,
                                     w_br_da, w_br_gla, w_br_xa, w_out, ffn_norm_w, peer_w_q, peer_sub_keys, peer_u,
                                     peer_v)))
    return _forward(x, mem, params)
```

```python
import functools
import math

import jax
import jax.numpy as jnp
import numpy as np
from jax import lax
from jax.experimental import pallas as pl
from jax.experimental.pallas import tpu as pltpu

F32 = jnp.float32
BF16 = jnp.bfloat16

LANES = 128
SUBLANES = 8
VMEM_LIMIT_BYTES = 56 * 1024 * 1024

D_MODEL = 1024
DA_HEADS = 4
DA_HEAD_DIM = 64
DA_V_DIM = 2 * DA_HEAD_DIM
GLA_HEADS = 4
GLA_DK = 64
GLA_DV = 128
GLA_GATE_RANK = 16
GLA_TAU = 16.0
XA_HEADS = 4
XA_HEAD_DIM = 128
N_BRANCH = 3
PEER_HEADS = 8
PEER_N_KEYS = 128
PEER_QDIM = 256
PEER_TOPK = 16
RMS_EPS = 1e-6
LOG2E = math.log2(math.e)

DA_W = DA_HEADS * 2 * DA_HEAD_DIM
GLA_QW = GLA_HEADS * GLA_DK
GLA_VW = GLA_HEADS * GLA_DV
XA_W = XA_HEADS * XA_HEAD_DIM
GATE_W = N_BRANCH * D_MODEL
PEER_SLOTS = PEER_HEADS * PEER_TOPK

C_DAQ = 0
C_DAK = C_DAQ + DA_W
C_DAV = C_DAK + DA_W
C_GLQ = C_DAV + DA_W
C_GLK = C_GLQ + GLA_QW
C_GLV = C_GLK + GLA_QW
C_GLR = C_GLV + GLA_VW
C_XAQ = C_GLR + GLA_VW
C_GATE = C_XAQ + XA_W
C_GLG = C_GATE + GATE_W
C_END = C_GLG + LANES

GLA_CHUNK = 128
GLA_LEVELS = 7
GLA_CHUNKS_PER_STEP = 4

ROW_SUBLANES = 4
ROW_PIECES = 2 * ROW_SUBLANES
PEER_TOKENS_PER_STEP = 128


def _cparams(sem):
    return pltpu.CompilerParams(dimension_semantics=sem, vmem_limit_bytes=VMEM_LIMIT_BYTES)


def _resident(shape):
    nd = len(shape)
    return pl.BlockSpec(shape, lambda *_: (0,) * nd, pipeline_mode=pl.Buffered(1))


def _dot(a, b):
    return jnp.dot(a, b, preferred_element_type=F32)


def _dot_nt(a, b):
    return lax.dot_general(a, b, (((1,), (1,)), ((), ())), preferred_element_type=F32)


def _split2(t):
    hi = t.astype(BF16)
    lo = (t - hi.astype(F32)).astype(BF16)
    return hi, lo


def _group_sumsq(t, bd_ref):
    hi, lo = _split2(t * t)
    bd = bd_ref[...]
    return _dot(hi, bd) + _dot(lo, bd)


def _in_proj_body(x_ref, nw_ref, w_ref, wvt_ref, qw_ref, kw_ref, xw_ref, wg_ref, bg_ref, bd64_ref, bd128_ref,
                  daq_ref, dak_ref, dav_ref, glq_ref, glk_ref, glv_ref, glvt_ref, glr_ref, gla_ref, xaq_ref,
                  gate_ref, *, seq):
    x = x_ref[...]
    ms = jnp.mean(x * x, axis=-1, keepdims=True)
    xn = (x * lax.rsqrt(ms + RMS_EPS) * nw_ref[...]).astype(BF16)

    def proj(c0, width):
        return _dot(xn, w_ref[:, c0:c0 + width])

    q = proj(C_DAQ, DA_W)
    q = q * lax.rsqrt(_group_sumsq(q, bd64_ref) * (1.0 / DA_HEAD_DIM) + RMS_EPS) * qw_ref[...]
    k = proj(C_DAK, DA_W)
    k = k * lax.rsqrt(_group_sumsq(k, bd64_ref) * (1.0 / DA_HEAD_DIM) + RMS_EPS) * kw_ref[...]
    tm = x.shape[0]
    lane = lax.broadcasted_iota(jnp.int32, (1, LANES), 1)
    pos = (pl.program_id(0) * tm + lax.broadcasted_iota(jnp.int32, (tm, 1), 0)) % seq
    pos = pos.astype(F32)
    for h in range(DA_HEADS):
        hs = slice(h * DA_V_DIM, (h + 1) * DA_V_DIM)
        b0 = pos * (LOG2E * 2.0 ** (-8.0 * (h + 1) / DA_HEADS))
        p0 = b0.astype(BF16).astype(F32)
        b1 = b0 - p0
        p1 = b1.astype(BF16).astype(F32)
        p2 = b1 - p1
        for c in range(2):
            own = (lane < DA_HEAD_DIM) if c == 0 else (lane >= DA_HEAD_DIM)
            first = DA_HEAD_DIM * (1 - c)
            cs = slice((2 * h + c) * LANES, (2 * h + c + 1) * LANES)
            ones = ((lane >= first) & (lane < first + 3)).astype(F32)
            daq_ref[:, cs] = jnp.where(own, q[:, hs], ones).astype(BF16)
            feat = jnp.where(lane == first, p0, jnp.where(lane == first + 1, p1, jnp.where(lane == first + 2, p2, 0.0)))
            dak_ref[:, cs] = jnp.where(own, k[:, hs], feat).astype(BF16)
    dav_ref[...] = proj(C_DAV, DA_W).astype(BF16)

    glq_ref[...] = proj(C_GLQ, GLA_QW) * (GLA_DK ** -0.5)
    glk_ref[...] = proj(C_GLK, GLA_QW)
    glv_ref[...] = proj(C_GLV, GLA_VW).astype(BF16)
    glvt_ref[...] = _dot_nt(wvt_ref[...], xn).astype(BF16)
    r = proj(C_GLR, GLA_VW)
    glr_ref[...] = (r * jax.nn.sigmoid(r)).astype(BF16)

    g = proj(C_GLG, LANES)
    z = jnp.dot(g, wg_ref[...], preferred_element_type=F32, precision=lax.Precision.HIGHEST) + bg_ref[...]
    log_sig = jnp.minimum(z, 0.0) - jnp.log1p(jnp.exp(-jnp.abs(z)))
    gla_ref[...] = log_sig * (1.0 / GLA_TAU)

    xq = proj(C_XAQ, XA_W)
    xaq_ref[...] = (xq * lax.rsqrt(_group_sumsq(xq, bd128_ref) * (1.0 / XA_HEAD_DIM) + RMS_EPS) * xw_ref[...]).astype(BF16)

    for c in range(0, GATE_W, 512):
        gate_ref[:, c:c + 512] = jax.nn.sigmoid(proj(C_GATE + c, 512)).astype(BF16)


def _block_diag_ones(width, group):
    idx = np.arange(width) // group
    return jnp.asarray((idx[:, None] == idx[None, :]).astype(np.float32), dtype=BF16)


def _in_proj(x2, mix_norm_w, w_in, da_q_norm_w, da_k_norm_w, xa_q_norm_w, gla_w_gate, gla_b_gate, *, seq, tm,
             interpret):
    n = x2.shape[0]
    c_glg_src = C_GLR + GLA_VW
    w = jnp.concatenate(
        [w_in[:, :c_glg_src], w_in[:, c_glg_src + GLA_GATE_RANK:], w_in[:, c_glg_src:c_glg_src + GLA_GATE_RANK],
         jnp.zeros((D_MODEL, LANES - GLA_GATE_RANK), w_in.dtype)], axis=1).astype(BF16)
    wvt = w_in[:, C_GLV:C_GLV + GLA_VW].T.astype(BF16)
    qw = jnp.tile(da_q_norm_w, 2 * DA_HEADS)[None, :] * (DA_HEAD_DIM ** -0.5 * LOG2E)
    kw = jnp.tile(da_k_norm_w, 2 * DA_HEADS)[None, :]
    xw = jnp.tile(xa_q_norm_w, XA_HEADS)[None, :] * (XA_HEAD_DIM ** -0.5)
    wg = jnp.zeros((LANES, GLA_QW), F32).at[:GLA_GATE_RANK].set(gla_w_gate)
    bg = gla_b_gate[None, :]
    row = lambda width: pl.BlockSpec((tm, width), lambda i: (i, 0))
    out_widths = [(2 * DA_W, BF16), (2 * DA_W, BF16), (DA_W, BF16), (GLA_QW, F32), (GLA_QW, F32), (GLA_VW, BF16)]
    out_shape = [jax.ShapeDtypeStruct((n, wd), dt) for wd, dt in out_widths]
    out_specs = [row(wd) for wd, _ in out_widths]
    out_shape.append(jax.ShapeDtypeStruct((GLA_VW, n), BF16))
    out_specs.append(pl.BlockSpec((GLA_VW, tm), lambda i: (0, i)))
    for wd, dt in [(GLA_VW, BF16), (GLA_QW, F32), (XA_W, BF16), (GATE_W, BF16)]:
        out_shape.append(jax.ShapeDtypeStruct((n, wd), dt))
        out_specs.append(row(wd))
    return pl.pallas_call(
        functools.partial(_in_proj_body, seq=seq),
        grid=(n // tm,),
        in_specs=[row(D_MODEL), _resident((1, D_MODEL)), _resident((D_MODEL, C_END)), _resident((GLA_VW, D_MODEL)),
                  _resident((1, DA_W)), _resident((1, DA_W)), _resident((1, XA_W)), _resident((LANES, GLA_QW)),
                  _resident((1, GLA_QW)), _resident((DA_W, DA_W)), _resident((XA_W, XA_W))],
        out_specs=out_specs,
        out_shape=out_shape,
        compiler_params=_cparams(("parallel",)),
        name="in_proj",
        interpret=interpret,
    )(x2, mix_norm_w[None, :], w, wvt, qw, kw, xw, wg, bg, _block_diag_ones(DA_W, DA_HEAD_DIM),
      _block_diag_ones(XA_W, XA_HEAD_DIM))


def _diff_attn_body(qi_tab, kj_tab, q_ref, k_ref, v_ref, lam_ref, ow_ref, o_ref, m_scr, acc_scr, *, blk_q, blk_k,
                    lam_init):
    p = pl.program_id(1)
    qi = qi_tab[p]
    kj = kj_tab[p]
    last = (qi * blk_q) // blk_k

    @pl.when(kj == 0)
    def _():
        m_scr[...] = jnp.full(m_scr.shape, -jnp.inf, F32)
        acc_scr[...] = jnp.zeros(acc_scr.shape, F32)

    def component(h, c, causal):
        v = v_ref[:, h * DA_V_DIM:(h + 1) * DA_V_DIM]
        v_ones = jnp.concatenate([v, jnp.ones_like(v)], axis=1)
        cs = slice((2 * h + c) * LANES, (2 * h + c + 1) * LANES)
        s = _dot_nt(q_ref[:, cs], k_ref[:, cs])
        if causal:
            row = lax.broadcasted_iota(jnp.int32, (blk_q, 1), 0) + qi * blk_q
            col = lax.broadcasted_iota(jnp.int32, (1, blk_k), 1) + kj * blk_k
            s = jnp.where(col <= row, s, -jnp.inf)
        i = 2 * h + c
        m_prev = m_scr[i]
        m_new = jnp.maximum(m_prev, jnp.max(s, axis=-1, keepdims=True))
        acc_scr[i] = jnp.exp2(m_prev - m_new) * acc_scr[i] + _dot(jnp.exp2(s - m_new).astype(BF16), v_ones)
        m_scr[i] = m_new

    @pl.when(kj < last)
    def _():
        for h in range(DA_HEADS):
            component(h, 0, False)
            component(h, 1, False)

    @pl.when(kj == last)
    def _():
        lv = lam_ref[...]
        lam = (jnp.exp(jnp.sum(lv[0:1] * lv[1:2], axis=-1, keepdims=True))
               - jnp.exp(jnp.sum(lv[2:3] * lv[3:4], axis=-1, keepdims=True)) + lam_init)
        for h in range(DA_HEADS):
            component(h, 0, True)
            component(h, 1, True)
            a0 = acc_scr[2 * h]
            a1 = acc_scr[2 * h + 1]
            o = a0[:, :DA_V_DIM] / a0[:, DA_V_DIM:] - lam * (a1[:, :DA_V_DIM] / a1[:, DA_V_DIM:])
            ms = jnp.mean(o * o, axis=-1, keepdims=True)
            o_ref[:, h * DA_V_DIM:(h + 1) * DA_V_DIM] = (
                o * lax.rsqrt(ms + RMS_EPS) * ow_ref[...] * (1.0 - lam_init)).astype(BF16)


def _diff_attn(daq, dak, dav, da_lambda, da_out_norm_w, *, batch, seq, blk_q, blk_k, lam_init, interpret):
    assert blk_k % blk_q == 0 and seq % blk_k == 0
    nq = seq // blk_q
    nk = seq // blk_k
    pairs = [(qi, kj) for qi in range(nq) for kj in range((qi * blk_q) // blk_k + 1)]
    qi_tab = jnp.asarray([p[0] for p in pairs], jnp.int32)
    kj_tab = jnp.asarray([p[1] for p in pairs], jnp.int32)
    qmap = lambda b, p, qt, kt: (b * nq + qt[p], 0)
    kmap = lambda b, p, qt, kt: (b * nk + kt[p], 0)
    n_comp = 2 * DA_HEADS
    grid_spec = pltpu.PrefetchScalarGridSpec(
        num_scalar_prefetch=2,
        grid=(batch, len(pairs)),
        in_specs=[pl.BlockSpec((blk_q, n_comp * LANES), qmap), pl.BlockSpec((blk_k, n_comp * LANES), kmap),
                  pl.BlockSpec((blk_k, DA_HEADS * DA_V_DIM), kmap),
                  pl.BlockSpec((4, DA_HEAD_DIM), lambda b, p, qt, kt: (0, 0)),
                  pl.BlockSpec((1, DA_V_DIM), lambda b, p, qt, kt: (0, 0))],
        out_specs=pl.BlockSpec((blk_q, DA_HEADS * DA_V_DIM), qmap),
        scratch_shapes=[pltpu.VMEM((n_comp, blk_q, 1), F32), pltpu.VMEM((n_comp, blk_q, 2 * DA_V_DIM), F32)],
    )
    return pl.pallas_call(
        functools.partial(_diff_attn_body, blk_q=blk_q, blk_k=blk_k, lam_init=lam_init),
        grid_spec=grid_spec,
        out_shape=jax.ShapeDtypeStruct((batch * seq, DA_HEADS * DA_V_DIM), BF16),
        compiler_params=_cparams(("parallel", "arbitrary")),
        name="diff_attn",
        interpret=interpret,
    )(qi_tab, kj_tab, daq, dak, dav, da_lambda, da_out_norm_w[None, :])


def _gla_constants():
    c = GLA_CHUNK
    t = np.arange(c)[:, None]
    u = np.arange(c)[None, :]
    lmats, masks = [], []
    for lev in range(GLA_LEVELS):
        m = 1 << lev
        second = (t % (2 * m)) >= m
        first = ~second
        bnd = (t // m) * m
        lmats.append(second & (u >= bnd) & (u <= t))
    for lev in range(GLA_LEVELS):
        m = 1 << lev
        first = (t % (2 * m)) < m
        end = (t // m) * m + m - 1
        lmats.append(first & (u > t) & (u <= end))
    lmats.append(u <= t)
    lmats.append(u > t)
    for lev in range(GLA_LEVELS):
        m = 1 << lev
        s = np.arange(c)[None, :]
        masks.append(((t // (2 * m)) == (s // (2 * m))) & ((t % (2 * m)) >= m) & ((s % (2 * m)) < m))
    masks.append(t == np.arange(c)[None, :])
    lall = jnp.asarray(np.concatenate(lmats, axis=0).astype(np.float32), dtype=BF16)
    mall = jnp.asarray(np.stack(masks, axis=0).astype(np.float32))
    return lall, mall


def _gla_body(q_ref, k_ref, la_ref, v_ref, vt_ref, r_ref, lall_ref, mall_ref, ow_ref, o_ref, state_scr):
    c = GLA_CHUNK

    @pl.when(pl.program_id(1) == 0)
    def _():
        state_scr[...] = jnp.zeros(state_scr.shape, F32)

    lall = lall_ref[...]
    lane = lax.broadcasted_iota(jnp.int32, (1, GLA_QW), 1)
    head_masks = [(lane // GLA_DK == h).astype(F32) for h in range(GLA_HEADS)]

    def intra(rs):
        q = q_ref[rs, :]
        k = k_ref[rs, :]
        g = la_ref[rs, :]
        g1 = g.astype(BF16)
        r1 = g - g1.astype(F32)
        g2 = r1.astype(BF16)
        g3 = (r1 - g2.astype(F32)).astype(BF16)
        e_all = jnp.exp(_dot(lall, g1) + _dot(lall, g2) + _dot(lall, g3))

        def rows(i):
            return e_all[i * c:(i + 1) * c]

        a = [jnp.zeros((c, c), F32) for _ in range(GLA_HEADS)]
        for lev in range(GLA_LEVELS + 1):
            if lev < GLA_LEVELS:
                ql = q * rows(lev)
                kl = (k * rows(GLA_LEVELS + lev)).astype(BF16)
            else:
                ql = q
                kl = k.astype(BF16)
            mask = mall_ref[lev]
            for h in range(GLA_HEADS):
                a[h] = a[h] + mask * _dot_nt((ql * head_masks[h]).astype(BF16), kl)
        e_b = rows(2 * GLA_LEVELS)
        return a, q * e_b, (k * rows(2 * GLA_LEVELS + 1)).astype(BF16), e_b[c - 1:c, :]

    chunks = [slice(i * c, (i + 1) * c) for i in range(GLA_CHUNKS_PER_STEP)]
    parts = [intra(rs) for rs in chunks]
    for rs, (a, q_dec, k_dec, chunk_decay) in zip(chunks, parts):
        for h in range(GLA_HEADS):
            vs = slice(h * GLA_DV, (h + 1) * GLA_DV)
            st = state_scr[h]
            o = (_dot(a[h].astype(BF16), v_ref[rs, vs])
                 + _dot_nt((q_dec * head_masks[h]).astype(BF16), st.astype(BF16)))
            state_scr[h] = st * chunk_decay + _dot(vt_ref[vs, rs], k_dec)
            ms = jnp.mean(o * o, axis=-1, keepdims=True)
            o_ref[rs, vs] = (o * lax.rsqrt(ms + RMS_EPS) * ow_ref[...] * r_ref[rs, vs].astype(F32)).astype(BF16)


def _gla(glq, glk, gla, glv, glvt, glr, gla_out_norm_w, *, batch, seq, interpret):
    rows_per_step = GLA_CHUNK * GLA_CHUNKS_PER_STEP
    ns = seq // rows_per_step
    lall, mall = _gla_constants()
    row = lambda width: pl.BlockSpec((rows_per_step, width), lambda b, i: (b * ns + i, 0))
    return pl.pallas_call(
        _gla_body,
        grid=(batch, ns),
        in_specs=[row(GLA_QW), row(GLA_QW), row(GLA_QW), row(GLA_VW),
                  pl.BlockSpec((GLA_VW, rows_per_step), lambda b, i: (0, b * ns + i)), row(GLA_VW),
                  _resident(lall.shape), _resident(mall.shape), _resident((1, GLA_DV))],
        out_specs=row(GLA_VW),
        out_shape=jax.ShapeDtypeStruct((batch * seq, GLA_VW), BF16),
        scratch_shapes=[pltpu.VMEM((GLA_HEADS, GLA_DV, GLA_QW), F32)],
        compiler_params=_cparams(("parallel", "arbitrary")),
        name="gla",
        interpret=interpret,
    )(glq, glk, gla, glv, glvt, glr, lall, mall, gla_out_norm_w[None, :])


def _mem_kv_body(mem_ref, nw_ref, w_ref, kw_ref, bd_ref, mk_ref, mv_ref):
    x = mem_ref[...]
    ms = jnp.mean(x * x, axis=-1, keepdims=True)
    xn = (x * lax.rsqrt(ms + RMS_EPS) * nw_ref[...]).astype(BF16)
    k = _dot(xn, w_ref[:, :XA_W])
    mk_ref[...] = (k * lax.rsqrt(_group_sumsq(k, bd_ref) * (1.0 / XA_HEAD_DIM) + RMS_EPS) * kw_ref[...]).astype(BF16)
    mv_ref[...] = _dot(xn, w_ref[:, XA_W:]).astype(BF16)


def _mem_kv(mem2, mem_norm_w, w_mem_kv, xa_k_norm_w, *, batch, mem_len, interpret):
    blk = pl.BlockSpec((mem_len, XA_W), lambda b: (b, 0))
    return pl.pallas_call(
        _mem_kv_body,
        grid=(batch,),
        in_specs=[pl.BlockSpec((mem_len, D_MODEL), lambda b: (b, 0)), _resident((1, D_MODEL)),
                  _resident((D_MODEL, 2 * XA_W)), _resident((1, XA_W)), _resident((XA_W, XA_W))],
        out_specs=[blk, blk],
        out_shape=[jax.ShapeDtypeStruct((batch * mem_len, XA_W), BF16)] * 2,
        compiler_params=_cparams(("parallel",)),
        name="mem_kv",
        interpret=interpret,
    )(mem2, mem_norm_w[None, :], w_mem_kv.astype(BF16), jnp.tile(xa_k_norm_w, XA_HEADS)[None, :],
      _block_diag_ones(XA_W, XA_HEAD_DIM))


def _piece_offset(j):
    return (j % 2) * (D_MODEL // 2) + (j // 2) * LANES


def _merge_body(x_ref, oda_ref, ogla_ref, xaq_ref, gate_ref, mk_ref, mv_ref, wda_ref, wgla_ref, wxa_ref, wout_ref,
                fw_ref, wq_ref, keys_ref, pair_code_ref, x1_ref, x8_ref, idx_ref, g_ref):
    tm = x_ref.shape[0]
    br_xa = None
    for h in range(XA_HEADS):
        hs = slice(h * XA_HEAD_DIM, (h + 1) * XA_HEAD_DIM)
        s = _dot_nt(xaq_ref[:, hs], mk_ref[:, hs])
        s = s - jnp.max(s, axis=-1, keepdims=True)
        p = jnp.exp(s)
        p = p / jnp.sum(p, axis=-1, keepdims=True)
        o = _dot(p.astype(BF16), mv_ref[:, hs])
        t = _dot(o.astype(BF16), wxa_ref[hs, :])
        br_xa = t if br_xa is None else br_xa + t
    merged = (gate_ref[:, 0:D_MODEL].astype(F32) * _dot(oda_ref[...], wda_ref[...])
              + gate_ref[:, D_MODEL:2 * D_MODEL].astype(F32) * _dot(ogla_ref[...], wgla_ref[...])
              + gate_ref[:, 2 * D_MODEL:].astype(F32) * br_xa)
    x1 = x_ref[...] + _dot(merged.astype(BF16), wout_ref[...])
    x1_ref[...] = x1
    ms = jnp.mean(x1 * x1, axis=-1, keepdims=True)
    xn = x1 * lax.rsqrt(ms + RMS_EPS) * fw_ref[...]
    xnb = xn.astype(BF16)
    for j in range(ROW_PIECES):
        off = _piece_offset(j)
        x8_ref[pl.ds(j, tm, stride=ROW_PIECES), :] = xn[:, off:off + LANES]
    pq = _dot(xnb, wq_ref[...])
    half = PEER_QDIM // 2

    def score(hp):
        q_hi, q_lo = _split2(pq[:, hp * half:(hp + 1) * half])
        keys = keys_ref[hp % 2]
        return _dot_nt(keys, q_hi) + _dot_nt(keys, q_lo)

    _topk_slots(score, pair_code_ref, idx_ref, g_ref)


def _merge(x2, oda, ogla, xaq, gates, mk, mv, w_br_da, w_br_gla, w_br_xa, w_out, ffn_norm_w, peer_w_q, peer_sub_keys,
           *, seq, mem_len, tm, interpret):
    n = x2.shape[0]
    spb = seq // tm
    row = lambda width: pl.BlockSpec((tm, width), lambda i: (i, 0))
    mem_blk = pl.BlockSpec((mem_len, XA_W), lambda i: (i // spb, 0))
    pair_code = jnp.asarray(_pair_candidates()[1])
    return pl.pallas_call(
        _merge_body,
        grid=(n // tm,),
        in_specs=[row(D_MODEL), row(DA_W), row(GLA_VW), row(XA_W), row(GATE_W), mem_blk, mem_blk,
                  _resident((DA_W, D_MODEL)), _resident((GLA_VW, D_MODEL)), _resident((XA_W, D_MODEL)),
                  _resident((D_MODEL, D_MODEL)), _resident((1, D_MODEL)),
                  _resident((D_MODEL, PEER_HEADS * PEER_QDIM)), _resident((2, PEER_N_KEYS, PEER_QDIM // 2)),
                  _resident(pair_code.shape)],
        out_specs=[row(D_MODEL), pl.BlockSpec((tm * ROW_PIECES, LANES), lambda i: (i, 0)),
                   pl.BlockSpec((PEER_SLOTS, tm), lambda i: (0, i)), row(PEER_SLOTS)],
        out_shape=[jax.ShapeDtypeStruct((n, D_MODEL), F32), jax.ShapeDtypeStruct((n * ROW_PIECES, LANES), F32),
                   jax.ShapeDtypeStruct((PEER_SLOTS, n), jnp.int32), jax.ShapeDtypeStruct((n, PEER_SLOTS), F32)],
        compiler_params=_cparams(("parallel",)),
        name="merge",
        interpret=interpret,
    )(x2, oda, ogla, xaq, gates, mk, mv, w_br_da.astype(BF16), w_br_gla.astype(BF16), w_br_xa.astype(BF16),
      w_out.astype(BF16), ffn_norm_w[None, :], peer_w_q.astype(BF16), peer_sub_keys.astype(BF16), pair_code)


def _top16(x, code):
    vals, codes = [], []
    for _ in range(PEER_TOPK):
        m = jnp.max(x, axis=0, keepdims=True)
        win = jnp.min(jnp.where(x == m, code, jnp.inf), axis=0, keepdims=True)
        vals.append(m)
        codes.append(win)
        x = jnp.where(code == win, -jnp.inf, x)
    return jnp.concatenate(vals, axis=0), jnp.concatenate(codes, axis=0).astype(jnp.int32)


def _select_rows(table, pos):
    out = jnp.zeros(pos.shape, table.dtype)
    for i in range(PEER_TOPK):
        out = jnp.where(pos == i, table[i:i + 1, :], out)
    return out


def _pair_candidates():
    blocks = [(0, 1, 0, PEER_TOPK)] + [(i, i + 1, 0, SUBLANES) for i in range(1, SUBLANES)]
    blocks.append((SUBLANES, PEER_TOPK, 0, 1))
    codes = np.concatenate([np.array([i * PEER_TOPK + j for i in range(i0, i1) for j in range(j0, j1)])
                            for i0, i1, j0, j1 in blocks])
    return blocks, codes.astype(np.float32)[:, None]


def _topk_slots(score, pair_code_ref, idx_ref, g_ref):
    blocks, _ = _pair_candidates()
    tb = idx_ref.shape[1]
    key_code = lax.broadcasted_iota(jnp.int32, (PEER_N_KEYS, tb), 0).astype(F32)
    pair_code = jnp.broadcast_to(pair_code_ref[...], (pair_code_ref.shape[0], tb))
    experts, gates = [], []
    for h in range(PEER_HEADS):
        s0, i0 = _top16(score(2 * h), key_code)
        s1, i1 = _top16(score(2 * h + 1), key_code)
        cand = jnp.concatenate([s0[a0:a1, :] + s1[b0:b1, :] for a0, a1, b0, b1 in blocks], axis=0)
        best, flat = _top16(cand, pair_code)
        expert = (_select_rows(i0, jnp.right_shift(flat, 4)) * PEER_N_KEYS
                  + _select_rows(i1, jnp.bitwise_and(flat, PEER_TOPK - 1)))
        e = jnp.exp(best - best[0:1, :])
        odd = jnp.bitwise_and(lax.broadcasted_iota(jnp.int32, expert.shape, 0), 1)
        experts.append(expert * ROW_SUBLANES + (SUBLANES - ROW_SUBLANES * odd))
        gates.append(e / jnp.sum(e, axis=0, keepdims=True))
    idx_ref[...] = jnp.concatenate(experts, axis=0)
    g_ref[...] = jnp.concatenate(gates, axis=0).T


def _pack_table(t):
    e = t.shape[0]
    bits = lax.bitcast_convert_type(t.astype(BF16), jnp.uint16).astype(jnp.uint32)
    bits = bits.reshape(e, 2, ROW_SUBLANES, LANES)
    packed = (bits[:, 0] | (bits[:, 1] << 16)).reshape(e * ROW_SUBLANES, LANES)
    pad = jnp.zeros((SUBLANES, LANES), jnp.uint32)
    return jnp.concatenate([pad, packed, pad], axis=0)


def _expert_rows(idx_scr, tab_ref, t):
    low = lax.broadcasted_iota(jnp.int32, (SUBLANES, LANES), 0) < ROW_SUBLANES
    tiles = [tab_ref[pl.ds(pl.multiple_of(idx_scr.at[k][t], ROW_SUBLANES), SUBLANES), :] for k in range(PEER_SLOTS)]
    pairs = [jnp.where(low, tiles[2 * m], tiles[2 * m + 1]) for m in range(PEER_SLOTS // 2)]
    return pltpu.bitcast(jnp.concatenate(pairs, axis=0), BF16)


def _token_blocks(idx_hbm, idx_a, idx_b, sem, per_token, tb):
    i = pl.program_id(0)

    def copy(blk, scr, s):
        return pltpu.make_async_copy(idx_hbm.at[:, pl.ds(pl.multiple_of(blk * tb, LANES), tb)], scr, sem.at[s])

    def run_block(idx_scr, base):
        def body(j, carry):
            for u in range(PEER_TOKENS_PER_STEP):
                per_token(idx_scr, PEER_TOKENS_PER_STEP * j + u, base)
            return carry

        lax.fori_loop(0, tb // PEER_TOKENS_PER_STEP, body, 0)

    @pl.when(i == 0)
    def _():
        copy(0, idx_a, 0).start()

    copy(2 * i, idx_a, 0).wait()
    copy(2 * i + 1, idx_b, 1).start()
    run_block(idx_a, 0)
    copy(2 * i + 1, idx_b, 1).wait()

    @pl.when(i + 1 < pl.num_programs(0))
    def _():
        copy(2 * i + 2, idx_a, 0).start()

    run_block(idx_b, tb)


def _piece_mask():
    r = lax.broadcasted_iota(jnp.int32, (ROW_PIECES, PEER_SLOTS * ROW_PIECES), 0)
    c = lax.broadcasted_iota(jnp.int32, (ROW_PIECES, PEER_SLOTS * ROW_PIECES), 1)
    return (c % ROW_PIECES == r).astype(F32)


def _peer_u_body(idx_hbm, x8_ref, g_ref, tab_ref, sel_ref, selt_ref, w_ref, idx_a, idx_b, sem, r_scr, *, tb):
    dmask = _piece_mask()

    def per_token(idx_scr, t, base):
        x8 = x8_ref[pl.ds(pl.multiple_of((base + t) * ROW_PIECES, ROW_PIECES), ROW_PIECES), :].astype(BF16)
        r = _dot_nt(x8, _expert_rows(idx_scr, tab_ref, t))
        r_scr[pl.ds(base + t, 1), :] = jnp.sum(r * dmask, axis=0, keepdims=True)

    _token_blocks(idx_hbm, idx_a, idx_b, sem, per_token, tb)
    hi, lo = _split2(r_scr[...])
    s = _dot(hi, sel_ref[...]) + _dot(lo, sel_ref[...])
    act = 0.5 * s * (1.0 + lax.erf(s * (2.0 ** -0.5)))
    w = (g_ref[...] * act).astype(BF16)
    w_ref[...] = _dot(w, selt_ref[...])


def _peer_v_body(idx_hbm, w_ref, x1_ref, tab_ref, o_ref, idx_a, idx_b, sem, o8_scr, *, tb):
    dmask = _piece_mask()

    def per_token(idx_scr, t, base):
        lhs = (w_ref[pl.ds(base + t, 1), :] * dmask).astype(BF16)
        o8_scr[pl.ds(pl.multiple_of((base + t) * ROW_PIECES, ROW_PIECES), ROW_PIECES), :] = _dot(
            lhs, _expert_rows(idx_scr, tab_ref, t))

    _token_blocks(idx_hbm, idx_a, idx_b, sem, per_token, tb)
    for j in range(ROW_PIECES):
        cs = slice(_piece_offset(j), _piece_offset(j) + LANES)
        o_ref[:, cs] = x1_ref[:, cs] + o8_scr[pl.ds(j, 2 * tb, stride=ROW_PIECES), :]


def _idx_scratch(tb):
    return [pltpu.SMEM((PEER_SLOTS, tb), jnp.int32), pltpu.SMEM((PEER_SLOTS, tb), jnp.int32),
            pltpu.SemaphoreType.DMA((2,))]


def _peer_u(idx_t, x8, gates, table, *, tb, interpret):
    n = idx_t.shape[1]
    ts = 2 * tb
    sel_np = (np.arange(PEER_SLOTS * ROW_PIECES)[:, None] // ROW_PIECES == np.arange(PEER_SLOTS)[None, :])
    sel = jnp.asarray(sel_np.astype(np.float32), dtype=BF16)
    return pl.pallas_call(
        functools.partial(_peer_u_body, tb=tb),
        grid=(n // ts,),
        in_specs=[pl.BlockSpec(memory_space=pl.ANY),
                  pl.BlockSpec((ts * ROW_PIECES, LANES), lambda i: (i, 0)),
                  pl.BlockSpec((ts, PEER_SLOTS), lambda i: (i, 0)),
                  _resident(table.shape), _resident(sel.shape), _resident(sel.T.shape)],
        out_specs=pl.BlockSpec((ts, PEER_SLOTS * ROW_PIECES), lambda i: (i, 0)),
        out_shape=jax.ShapeDtypeStruct((n, PEER_SLOTS * ROW_PIECES), F32),
        scratch_shapes=_idx_scratch(tb) + [pltpu.VMEM((ts, PEER_SLOTS * ROW_PIECES), F32)],
        compiler_params=_cparams(("arbitrary",)),
        name="peer_u",
        interpret=interpret,
    )(idx_t, x8, gates, table, sel, sel.T)


def _peer_v(idx_t, w_exp, x1, table, *, tb, interpret):
    n = idx_t.shape[1]
    ts = 2 * tb
    return pl.pallas_call(
        functools.partial(_peer_v_body, tb=tb),
        grid=(n // ts,),
        in_specs=[pl.BlockSpec(memory_space=pl.ANY),
                  pl.BlockSpec((ts, PEER_SLOTS * ROW_PIECES), lambda i: (i, 0)),
                  pl.BlockSpec((ts, D_MODEL), lambda i: (i, 0)),
                  _resident(table.shape)],
        out_specs=pl.BlockSpec((ts, D_MODEL), lambda i: (i, 0)),
        out_shape=jax.ShapeDtypeStruct((n, D_MODEL), F32),
        scratch_shapes=_idx_scratch(tb) + [pltpu.VMEM((ts * ROW_PIECES, LANES), F32)],
        compiler_params=_cparams(("arbitrary",)),
        name="peer_v",
        interpret=interpret,
    )(idx_t, w_exp, x1, table)


def _layer(x2, mem2, p, *, layer, batch, seq, mem_len, tm, attn_blk_q, attn_blk_k, peer_tb, interpret):
    lam_init = 0.8 - 0.6 * math.exp(-0.3 * layer)
    (daq, dak, dav, glq, glk, glv, glvt, glr, gla, xaq, gates) = _in_proj(
        x2, p["mix_norm_w"], p["w_in"], p["da_q_norm_w"], p["da_k_norm_w"], p["xa_q_norm_w"], p["gla_w_gate"],
        p["gla_b_gate"], seq=seq, tm=tm, interpret=interpret)
    oda = _diff_attn(daq, dak, dav, p["da_lambda"], p["da_out_norm_w"], batch=batch, seq=seq, blk_q=attn_blk_q,
                     blk_k=attn_blk_k, lam_init=lam_init, interpret=interpret)
    ogla = _gla(glq, glk, gla, glv, glvt, glr, p["gla_out_norm_w"], batch=batch, seq=seq, interpret=interpret)
    mk, mv = _mem_kv(mem2, p["mem_norm_w"], p["w_mem_kv"], p["xa_k_norm_w"], batch=batch, mem_len=mem_len,
                     interpret=interpret)
    x1, x8, idx_t, peer_gates = _merge(x2, oda, ogla, xaq, gates, mk, mv, p["w_br_da"], p["w_br_gla"], p["w_br_xa"], p["w_out"],
                             p["ffn_norm_w"], p["peer_w_q"], p["peer_sub_keys"], seq=seq, mem_len=mem_len, tm=tm,
                             interpret=interpret)
    w_exp = _peer_u(idx_t, x8, peer_gates, _pack_table(p["peer_u"]), tb=peer_tb, interpret=interpret)
    return _peer_v(idx_t, w_exp, x1, _pack_table(p["peer_v"]), tb=peer_tb, interpret=interpret)


_PARAM_NAMES = ("mix_norm_w", "w_in", "da_q_norm_w", "da_k_norm_w", "da_lambda", "da_out_norm_w", "gla_w_gate",
                "gla_b_gate", "gla_out_norm_w", "mem_norm_w", "w_mem_kv", "xa_q_norm_w", "xa_k_norm_w", "w_br_da",
                "w_br_gla", "w_br_xa", "w_out", "ffn_norm_w", "peer_w_q", "peer_sub_keys", "peer_u", "peer_v")


def _forward(x, mem, params, *, tm=256, attn_blk_q=512, attn_blk_k=1024, peer_tb=128, interpret=False):
    batch, seq, d = x.shape
    mem_len = mem.shape[1]
    x2 = x.reshape(batch * seq, d)
    mem2 = mem.reshape(batch * mem_len, d)
    depth = params["w_in"].shape[0]
    for layer in range(depth):
        p = {name: params[name][layer] for name in _PARAM_NAMES}
        x2 = _layer(x2, mem2, p, layer=layer, batch=batch, seq=seq, mem_len=mem_len, tm=tm, attn_blk_q=attn_blk_q,
                    attn_blk_k=attn_blk_k, peer_tb=peer_tb, interpret=interpret)
    return x2.reshape(batch, seq, d)


def kernel(x, mem, mix_norm_w, w_in, da_q_norm_w, da_k_norm_w, da_lambda, da_out_norm_w, gla_w_gate, gla_b_gate,
           gla_out_norm_w, mem_norm_w, w_mem_kv, xa_q_norm_w, xa_k_norm_w, w_br_da, w_br_gla, w_br_xa, w_out,
           ffn_norm_w, peer_w_q, peer_sub_keys, peer_u, peer_v):
    params = dict(zip(_PARAM_NAMES, (mix_norm_w, w_in, da_q_norm_w, da_k_norm_w, da_lambda, da_out_norm_w, gla_w_gate,
                                     gla_b_gate, gla_out_norm_w, mem_norm_w, w_mem_kv, xa_q_norm_w, xa_k_norm_w,
                                     w_br_da, w_br_gla, w_br_xa, w_out, ffn_norm_w, peer_w_q, peer_sub_keys, peer_u,
                                     peer_v)))
    return _forward(x, mem, params)
```
